```python
import math
import jax
import jax.numpy as jnp
from jax import lax
import numpy as np

D_MODEL = 2048
BATCH = 2
SEQ = 4096
DEPTH = 2
DEC_BATCH = 8
DEC_SEQ = 8
PAST_LEN = 16384
PAGE_SIZE = 128

HEAD_DIM = 128
MIX_HEADS = D_MODEL // 256
MOBA_W = MIX_HEADS * HEAD_DIM
DIFF_DIM = HEAD_DIM // 2
DIFF_W = MIX_HEADS * HEAD_DIM
RNN_WIDTH = D_MODEL // 2
RNN_BLOCKS = MIX_HEADS
CONV_WIDTH = 4
RGLRU_C = 8.0
N_BRANCH = 3
BRANCH_WIDTH = D_MODEL // 2
IN_COLS = 3 * MOBA_W + 3 * DIFF_W + RNN_WIDTH + N_BRANCH * D_MODEL
MOBA_BLOCK = 256
MOBA_TOPK = 3
MOBA_Q_CHUNK = 32
DIFF_Q_CHUNK = 128
N_EXPERTS = 64
TOP_K = 8
N_GROUPS = 8
TOPK_GROUPS = 4
D_EXPERT = D_MODEL // 4
ROUTED_SCALE = 2.5
MOE_BLOCK = 128
EPS = 1e-6

kernel_name = 'hybrid_moba_diff_rglru_moe_decode_step'


def rms_norm(x, g):
    xf = x.astype(jnp.float32)
    y = xf * lax.rsqrt(jnp.mean(xf * xf, axis=-1, keepdims=True) + EPS)
    return (y * g.astype(jnp.float32)).astype(x.dtype)


def moba_attention(q, k, v, q_pos0):
    B, Sq, H, Dh = q.shape
    Sk = k.shape[1]
    n_full = Sk // MOBA_BLOCK
    topk = min(MOBA_TOPK, n_full)
    qc = math.gcd(Sq, MOBA_Q_CHUNK)
    scale = Dh ** -0.5
    pad = jnp.zeros((B, MOBA_BLOCK, H, Dh), k.dtype)
    k_pad = jnp.concatenate([k, pad], axis=1)
    v_pad = jnp.concatenate([v, pad], axis=1)
    if topk > 0:
        kb = k[:, :n_full * MOBA_BLOCK].reshape(B, n_full, MOBA_BLOCK, H, Dh)
        vb = v[:, :n_full * MOBA_BLOCK].reshape(B, n_full, MOBA_BLOCK, H, Dh)
        k_mean = jnp.mean(kb.astype(jnp.float32), axis=2)
        kb = kb.transpose(0, 3, 1, 2, 4)
        vb = vb.transpose(0, 3, 1, 2, 4)
        b_idx = jnp.arange(B)[:, None, None, None]
        h_idx = jnp.arange(H)[None, :, None, None]

    def chunk(ci):
        start = ci * qc
        p0 = q_pos0 + start
        qb = lax.dynamic_slice_in_dim(q, start, qc, axis=1)
        q_pos = p0 + jnp.arange(qc)
        own = p0 // MOBA_BLOCK
        bs = own * MOBA_BLOCK
        k_own = lax.dynamic_slice_in_dim(k_pad, bs, MOBA_BLOCK, axis=1)
        v_own = lax.dynamic_slice_in_dim(v_pad, bs, MOBA_BLOCK, axis=1)
        k_pos = bs + jnp.arange(MOBA_BLOCK)
        s_own = jnp.einsum('bqhd,bkhd->bhqk', qb, k_own).astype(jnp.float32) * scale
        s_own = jnp.where(k_pos[None, :] <= q_pos[:, None], s_own, -jnp.inf)
        if topk == 0:
            p_own = jax.nn.softmax(s_own, axis=-1).astype(v.dtype)
            return jnp.einsum('bhqk,bkhd->bqhd', p_own, v_own)
        gate = jnp.einsum('bqhd,bnhd->bhqn', qb.astype(jnp.float32), k_mean)
        gate = jnp.where(jnp.arange(n_full) < own, gate, -jnp.inf)
        g_val, g_idx = lax.top_k(gate, topk)
        k_sel = kb[b_idx, h_idx, g_idx]
        v_sel = vb[b_idx, h_idx, g_idx]
        s_sel = jnp.einsum('bqhd,bhqjkd->bhqjk', qb, k_sel).astype(jnp.float32) * scale
        s_sel = jnp.where(jnp.isfinite(g_val)[..., None], s_sel, -jnp.inf)
        s = jnp.concatenate([s_sel.reshape(B, H, qc, topk * MOBA_BLOCK), s_own], axis=-1)
        pr = jax.nn.softmax(s, axis=-1).astype(v.dtype)
        p_sel = pr[..., :topk * MOBA_BLOCK].reshape(B, H, qc, topk, MOBA_BLOCK)
        p_own = pr[..., topk * MOBA_BLOCK:]
        return (jnp.einsum('bhqjk,bhqjkd->bqhd', p_sel, v_sel)
                + jnp.einsum('bhqk,bkhd->bqhd', p_own, v_own))

    out = lax.map(chunk, jnp.arange(Sq // qc))
    return out.transpose(1, 0, 2, 3, 4).reshape(B, Sq, H, Dh)


def diff_attention(q1, q2, k1, k2, v, q_pos0, lam):
    B, Sq, H, d = q1.shape
    Sk = k1.shape[1]
    qc = math.gcd(Sq, DIFF_Q_CHUNK)
    scale = d ** -0.5
    k_pos = jnp.arange(Sk)

    def chunk(ci):
        start = ci * qc
        a1 = lax.dynamic_slice_in_dim(q1, start, qc, axis=1)
        a2 = lax.dynamic_slice_in_dim(q2, start, qc, axis=1)
        q_pos = q_pos0 + start + jnp.arange(qc)
        mask = k_pos[None, :] <= q_pos[:, None]

        def probs(a, kk):
            s = jnp.einsum('bqhd,bkhd->bhqk', a, kk).astype(jnp.float32) * scale
            return jax.nn.softmax(jnp.where(mask, s, -jnp.inf), axis=-1)

        w = probs(a1, k1) - lam * probs(a2, k2)
        return jnp.einsum('bhqk,bkhd->bqhd', w.astype(v.dtype), v)

    out = lax.map(chunk, jnp.arange(Sq // qc))
    return out.transpose(1, 0, 2, 3, 4).reshape(B, Sq, H, v.shape[-1])


def rglru(x, h0, conv0, conv_w, conv_b, w_a, b_a, w_x, b_x, lam):
    B, S, W = x.shape
    xc = jnp.concatenate([conv0.astype(x.dtype), x], axis=1)
    u = conv_b + sum(xc[:, j:j + S] * conv_w[j] for j in range(CONV_WIDTH))
    ub = u.reshape(B, S, RNN_BLOCKS, W // RNN_BLOCKS)
    r = jax.nn.sigmoid((jnp.einsum('bsnk,nkj->bsnj', ub, w_a).reshape(B, S, W) + b_a).astype(jnp.float32))
    i = jax.nn.sigmoid((jnp.einsum('bsnk,nkj->bsnj', ub, w_x).reshape(B, S, W) + b_x).astype(jnp.float32))
    log_a = RGLRU_C * r * jax.nn.log_sigmoid(lam.astype(jnp.float32))
    a = jnp.exp(log_a)
    b = jnp.sqrt(-jnp.expm1(2.0 * log_a)) * (i * u.astype(jnp.float32))
    b = b.at[:, 0].add(a[:, 0] * h0.astype(jnp.float32))

    def combine(e1, e2):
        a1, b1 = e1
        a2, b2 = e2
        return a2 * a1, a2 * b1 + b2

    _, h = lax.associative_scan(combine, (a, b), axis=1)
    return h.astype(x.dtype), h[:, -1].astype(x.dtype), xc[:, S:]


def route(t, router_w, router_bias):
    T = t.shape[0]
    scores = jax.nn.sigmoid(t.astype(jnp.float32) @ router_w.astype(jnp.float32))
    sel = scores + router_bias.astype(jnp.float32)
    grp = sel.reshape(T, N_GROUPS, N_EXPERTS // N_GROUPS)
    grp_score = jnp.sum(lax.top_k(grp, 2)[0], axis=-1)
    _, g_idx = lax.top_k(grp_score, TOPK_GROUPS)
    g_mask = jnp.any(g_idx[..., None] == jnp.arange(N_GROUPS), axis=1)
    sel = jnp.where(jnp.repeat(g_mask, N_EXPERTS // N_GROUPS, axis=1), sel, -jnp.inf)
    _, e_idx = lax.top_k(sel, TOP_K)
    w = jnp.take_along_axis(scores, e_idx, axis=-1)
    w = w / jnp.sum(w, axis=-1, keepdims=True) * ROUTED_SCALE
    return e_idx, w


def routed_experts(t, e_idx, e_w, w_gate, w_up, w_down):
    T, D = t.shape
    A = T * TOP_K
    n_blocks = -(-A // MOE_BLOCK) + N_EXPERTS
    flat_e = e_idx.reshape(-1)
    order = jnp.argsort(flat_e)
    e_sorted = flat_e[order]
    counts = jnp.zeros((N_EXPERTS,), jnp.int32).at[flat_e].add(1)
    padded = (counts + MOE_BLOCK - 1) // MOE_BLOCK * MOE_BLOCK
    start = jnp.cumsum(counts) - counts
    p_end = jnp.cumsum(padded)
    p_start = p_end - padded
    dest = p_start[e_sorted] + (jnp.arange(A, dtype=jnp.int32) - start[e_sorted])
    slot_tok = jnp.full((n_blocks * MOE_BLOCK,), T, jnp.int32).at[dest].set((order // TOP_K).astype(jnp.int32))
    slot_w = jnp.zeros((n_blocks * MOE_BLOCK,), t.dtype).at[dest].set(e_w.reshape(-1)[order].astype(t.dtype))
    block_e = jnp.minimum(jnp.searchsorted(p_end, jnp.arange(n_blocks) * MOE_BLOCK, side='right'), N_EXPERTS - 1)
    t_pad = jnp.concatenate([t, jnp.zeros((1, D), t.dtype)], axis=0)

    def block(args):
        toks, e = args
        xb = t_pad[toks]
        return (jax.nn.silu(xb @ w_gate[e]) * (xb @ w_up[e])) @ w_down[e]

    yb = lax.map(block, (slot_tok.reshape(n_blocks, MOE_BLOCK), block_e))
    y = jax.ops.segment_sum(yb.reshape(-1, D) * slot_w[:, None], slot_tok, num_segments=T + 1)
    return y[:T]


def moe_ffn(h, router_w, router_bias, e_gate, e_up, e_down, s_gate, s_up, s_down):
    B, S, D = h.shape
    t = h.reshape(B * S, D)
    e_idx, e_w = route(t, router_w, router_bias)
    routed = routed_experts(t, e_idx, e_w, e_gate, e_up, e_down)
    shared = (jax.nn.silu(t @ s_gate) * (t @ s_up)) @ s_down
    return (routed + shared).reshape(B, S, D)


def layer(x, c, moba_past, diff_past, h0, conv0, lam_init, p):
    B, S, D = x.shape
    P = moba_past.shape[1]
    H, Dh = MIX_HEADS, HEAD_DIM
    mod = (jax.nn.silu(c) @ p['w_ada'] + p['b_ada'])[:, None, :]
    sh1, sc1, g1, sh2, sc2, g2 = jnp.split(mod, 6, axis=-1)
    h = rms_norm(x, p['norm1_g']) * (1 + sc1) + sh1
    cuts = [MOBA_W, 2 * MOBA_W, 3 * MOBA_W, 3 * MOBA_W + DIFF_W, 3 * MOBA_W + 2 * DIFF_W,
            3 * MOBA_W + 3 * DIFF_W, 3 * MOBA_W + 3 * DIFF_W + RNN_WIDTH]
    qa, ka, va, qd, kd, vd, xr, gl = jnp.split(h @ p['w_in'], cuts, axis=-1)
    qa = rms_norm(qa.reshape(B, S, H, Dh), p['moba_q_norm_g'])
    ka = rms_norm(ka.reshape(B, S, H, Dh), p['moba_k_norm_g'])
    kv_a = jnp.stack([ka, va.reshape(B, S, H, Dh)], axis=2)
    kv_a_all = jnp.concatenate([moba_past.astype(x.dtype), kv_a], axis=1)
    oa = moba_attention(qa, kv_a_all[:, :, 0], kv_a_all[:, :, 1], P)
    qd = rms_norm(qd.reshape(B, S, H, 2, DIFF_DIM), p['diff_q_norm_g'])
    kd = rms_norm(kd.reshape(B, S, H, 2, DIFF_DIM), p['diff_k_norm_g'])
    kv_d = jnp.stack([kd.reshape(B, S, H, Dh), vd.reshape(B, S, H, Dh)], axis=2)
    kv_d_all = jnp.concatenate([diff_past.astype(x.dtype), kv_d], axis=1)
    k_d_all = kv_d_all[:, :, 0].reshape(B, P + S, H, 2, DIFF_DIM)
    lq = p['diff_lambda'].astype(jnp.float32)
    lam = jnp.exp(jnp.sum(lq[0] * lq[1])) - jnp.exp(jnp.sum(lq[2] * lq[3])) + lam_init
    od = diff_attention(qd[..., 0, :], qd[..., 1, :], k_d_all[..., 0, :], k_d_all[..., 1, :],
                        kv_d_all[:, :, 1], P, lam)
    od = rms_norm(od, p['diff_subln_g']) * (1.0 - lam_init)
    orr, h_last, conv_last = rglru(xr, h0, conv0, p['rglru_conv_w'], p['rglru_conv_b'], p['rglru_wa'],
                                   p['rglru_ba'], p['rglru_wx'], p['rglru_bx'], p['rglru_lambda'])
    branches = jnp.stack([oa.reshape(B, S, MOBA_W), od.reshape(B, S, DIFF_W), orr], axis=2)
    gate = jax.nn.sigmoid(gl.reshape(B, S, N_BRANCH, D))
    merged = jnp.sum(jnp.einsum('bsnk,nkd->bsnd', branches, p['w_branch']) * gate, axis=2)
    x = x + g1 * (merged @ p['w_o'])
    h2 = rms_norm(x, p['norm2_g']) * (1 + sc2) + sh2
    x = x + g2 * moe_ffn(h2, p['router_w'], p['router_bias'], p['expert_w_gate'], p['expert_w_up'],
                         p['expert_w_down'], p['shared_w_gate'], p['shared_w_up'], p['shared_w_down'])
    return x, kv_a, kv_d, h_last, conv_last


def setup_inputs(seed: int = 0) -> dict:
    key = jax.random.key(seed)
    ks = iter(jax.random.split(key, 40))
    f32 = jnp.float32

    def nrm(shape, scale):
        return jax.random.normal(next(ks), shape, f32) * scale

    n_pages = PAST_LEN // PAGE_SIZE
    n_used = DEC_BATCH * n_pages
    n_pool = n_used + n_used // 4
    H, Dh = MIX_HEADS, HEAD_DIM
    d_in = D_MODEL ** -0.5
    x_prompt = nrm((BATCH, SEQ, D_MODEL), 1.0)
    x_sample = nrm((DEC_BATCH, DEC_SEQ, D_MODEL), 1.0)
    cache_moba_kv = nrm((DEPTH, n_pool, PAGE_SIZE, 2, H, Dh), 1.0)
    cache_diff_kv = nrm((DEPTH, n_pool, PAGE_SIZE, 2, H, Dh), 1.0)
    state_rglru_h = nrm((DEPTH, DEC_BATCH, RNN_WIDTH), 0.5)
    state_rglru_conv = nrm((DEPTH, DEC_BATCH, CONV_WIDTH - 1, RNN_WIDTH), 1.0)
    page_table = jax.random.permutation(next(ks), n_pool)[:n_used].reshape(DEC_BATCH, n_pages).astype(jnp.int32)
    c_prompt = nrm((BATCH, D_MODEL), 1.0)
    c_sample = nrm((DEC_BATCH, D_MODEL), 1.0)
    w_ada = nrm((DEPTH, D_MODEL, 6 * D_MODEL), 0.2 * d_in)
    b_ada = nrm((DEPTH, 6 * D_MODEL), 0.02)
    norm1_g = 1.0 + nrm((DEPTH, D_MODEL), 0.05)
    norm2_g = 1.0 + nrm((DEPTH, D_MODEL), 0.05)
    w_in = nrm((DEPTH, D_MODEL, IN_COLS), d_in)
    moba_q_norm_g = 1.0 + nrm((DEPTH, HEAD_DIM), 0.05)
    moba_k_norm_g = 1.0 + nrm((DEPTH, HEAD_DIM), 0.05)
    diff_q_norm_g = 1.0 + nrm((DEPTH, 2, DIFF_DIM), 0.05)
    diff_k_norm_g = 1.0 + nrm((DEPTH, 2, DIFF_DIM), 0.05)
    diff_lambda = nrm((DEPTH, 4, DIFF_DIM), 0.1)
    diff_subln_g = 1.0 + nrm((DEPTH, HEAD_DIM), 0.05)
    rglru_conv_w = nrm((DEPTH, CONV_WIDTH, RNN_WIDTH), CONV_WIDTH ** -0.5)
    rglru_conv_b = nrm((DEPTH, RNN_WIDTH), 0.02)
    bw = RNN_WIDTH // RNN_BLOCKS
    rglru_wa = nrm((DEPTH, RNN_BLOCKS, bw, bw), bw ** -0.5)
    rglru_ba = nrm((DEPTH, RNN_WIDTH), 0.02)
    rglru_wx = nrm((DEPTH, RNN_BLOCKS, bw, bw), bw ** -0.5)
    rglru_bx = nrm((DEPTH, RNN_WIDTH), 0.02)
    a0 = jax.random.uniform(next(ks), (DEPTH, RNN_WIDTH), f32, minval=0.9, maxval=0.999)
    rglru_lambda = jnp.log(a0) - jnp.log1p(-a0)
    w_branch = nrm((DEPTH, N_BRANCH, BRANCH_WIDTH, D_MODEL), BRANCH_WIDTH ** -0.5)
    w_o = nrm((DEPTH, D_MODEL, D_MODEL), d_in)
    router_w = nrm((DEPTH, D_MODEL, N_EXPERTS), d_in)
    router_bias = nrm((DEPTH, N_EXPERTS), 0.01)
    expert_w_gate = nrm((DEPTH, N_EXPERTS, D_MODEL, D_EXPERT), d_in)
    expert_w_up = nrm((DEPTH, N_EXPERTS, D_MODEL, D_EXPERT), d_in)
    expert_w_down = nrm((DEPTH, N_EXPERTS, D_EXPERT, D_MODEL), D_EXPERT ** -0.5)
    shared_w_gate = nrm((DEPTH, D_MODEL, D_EXPERT), d_in)
    shared_w_up = nrm((DEPTH, D_MODEL, D_EXPERT), d_in)
    shared_w_down = nrm((DEPTH, D_EXPERT, D_MODEL), D_EXPERT ** -0.5)
    return {'x_prompt': x_prompt, 'x_sample': x_sample, 'cache_moba_kv': cache_moba_kv,
            'cache_diff_kv': cache_diff_kv, 'state_rglru_h': state_rglru_h,
            'state_rglru_conv': state_rglru_conv, 'page_table': page_table,
            'c_prompt': c_prompt, 'c_sample': c_sample, 'w_ada': w_ada, 'b_ada': b_ada,
            'norm1_g': norm1_g, 'norm2_g': norm2_g, 'w_in': w_in,
            'moba_q_norm_g': moba_q_norm_g, 'moba_k_norm_g': moba_k_norm_g,
            'diff_q_norm_g': diff_q_norm_g, 'diff_k_norm_g': diff_k_norm_g,
            'diff_lambda': diff_lambda, 'diff_subln_g': diff_subln_g,
            'rglru_conv_w': rglru_conv_w, 'rglru_conv_b': rglru_conv_b,
            'rglru_wa': rglru_wa, 'rglru_ba': rglru_ba, 'rglru_wx': rglru_wx, 'rglru_bx': rglru_bx,
            'rglru_lambda': rglru_lambda, 'w_branch': w_branch, 'w_o': w_o,
            'router_w': router_w, 'router_bias': router_bias,
            'expert_w_gate': expert_w_gate, 'expert_w_up': expert_w_up, 'expert_w_down': expert_w_down,
            'shared_w_gate': shared_w_gate, 'shared_w_up': shared_w_up, 'shared_w_down': shared_w_down}


def reference(x_prompt, x_sample, cache_moba_kv, cache_diff_kv, state_rglru_h, state_rglru_conv,
              page_table, c_prompt, c_sample, w_ada, b_ada, norm1_g, norm2_g, w_in,
              moba_q_norm_g, moba_k_norm_g, diff_q_norm_g, diff_k_norm_g, diff_lambda, diff_subln_g,
              rglru_conv_w, rglru_conv_b, rglru_wa, rglru_ba, rglru_wx, rglru_bx, rglru_lambda,
              w_branch, w_o, router_w, router_bias, expert_w_gate, expert_w_up, expert_w_down,
              shared_w_gate, shared_w_up, shared_w_down):
    B = x_prompt.shape[0]
    DB = x_sample.shape[0]
    past_len = page_table.shape[1] * cache_moba_kv.shape[2]
    H, Dh = MIX_HEADS, HEAD_DIM
    no_past = jnp.zeros((B, 0, 2, H, Dh), x_prompt.dtype)
    h_zero = jnp.zeros((B, RNN_WIDTH), x_prompt.dtype)
    conv_zero = jnp.zeros((B, CONV_WIDTH - 1, RNN_WIDTH), x_prompt.dtype)
    y_prompt, y_sample = x_prompt, x_sample
    moba_p, moba_s, diff_p, diff_s, hp, hs, cvp, cvs = [], [], [], [], [], [], [], []
    for l in range(DEPTH):
        lam_init = 0.8 - 0.6 * math.exp(-0.3 * l)
        p = {'w_ada': w_ada[l], 'b_ada': b_ada[l], 'norm1_g': norm1_g[l], 'norm2_g': norm2_g[l],
             'w_in': w_in[l], 'moba_q_norm_g': moba_q_norm_g[l], 'moba_k_norm_g': moba_k_norm_g[l],
             'diff_q_norm_g': diff_q_norm_g[l], 'diff_k_norm_g': diff_k_norm_g[l],
             'diff_lambda': diff_lambda[l], 'diff_subln_g': diff_subln_g[l],
             'rglru_conv_w': rglru_conv_w[l], 'rglru_conv_b': rglru_conv_b[l],
             'rglru_wa': rglru_wa[l], 'rglru_ba': rglru_ba[l], 'rglru_wx': rglru_wx[l],
             'rglru_bx': rglru_bx[l], 'rglru_lambda': rglru_lambda[l],
             'w_branch': w_branch[l], 'w_o': w_o[l], 'router_w': router_w[l],
             'router_bias': router_bias[l], 'expert_w_gate': expert_w_gate[l],
             'expert_w_up': expert_w_up[l], 'expert_w_down': expert_w_down[l],
             'shared_w_gate': shared_w_gate[l], 'shared_w_up': shared_w_up[l],
             'shared_w_down': shared_w_down[l]}
        y_prompt, kva, kvd, hl, cl = layer(y_prompt, c_prompt, no_past, no_past, h_zero, conv_zero, lam_init, p)
        moba_p.append(kva)
        diff_p.append(kvd)
        hp.append(hl)
        cvp.append(cl)
        moba_past = cache_moba_kv[l][page_table].reshape(DB, past_len, 2, H, Dh)
        diff_past = cache_diff_kv[l][page_table].reshape(DB, past_len, 2, H, Dh)
        y_sample, kva, kvd, hl, cl = layer(y_sample, c_sample, moba_past, diff_past,
                                           state_rglru_h[l], state_rglru_conv[l], lam_init, p)
        moba_s.append(kva)
        diff_s.append(kvd)
        hs.append(hl)
        cvs.append(cl)
    return (y_prompt, y_sample, jnp.stack(moba_p), jnp.stack(moba_s), jnp.stack(diff_p), jnp.stack(diff_s),
            jnp.stack(hp), jnp.stack(hs), jnp.stack(cvp), jnp.stack(cvs))
```

```python
import functools
import math

import jax
import jax.numpy as jnp
from jax import lax
from jax.experimental import pallas as pl
from jax.experimental.pallas import tpu as pltpu

F32 = jnp.float32
BF16 = jnp.bfloat16
I32 = jnp.int32
EPS = 1e-6
NEG_INF = float("-inf")

LANES = 128
HEAD_DIM = 128
MOBA_BLOCK = 256
MOBA_TOPK = 3
RGLRU_C = 8.0
TOP_K = 8
N_GROUPS = 8
TOPK_GROUPS = 4
ROUTED_SCALE = 2.5
VMEM_LIMIT_BYTES = 56 * 1024 * 1024

_NT = (((1,), (1,)), ((), ()))


def _params(n_axes):
    return pltpu.CompilerParams(dimension_semantics=("arbitrary",) * n_axes,
                                vmem_limit_bytes=VMEM_LIMIT_BYTES)


def _tile(n, pref):
    return pref if n % pref == 0 else n


def _split_bf16(x):
    hi = x.astype(BF16)
    lo = (x - hi.astype(F32)).astype(BF16)
    return hi, lo


def _dot3(a, b, dims):
    ah, al = _split_bf16(a)
    bh, bl = _split_bf16(b)
    dg = functools.partial(lax.dot_general, dimension_numbers=dims, preferred_element_type=F32)
    return dg(ah, bh) + dg(ah, bl) + dg(al, bh)


def _group_ms(yc, gs):
    sq = yc * yc
    if gs == LANES:
        return jnp.mean(sq, axis=-1, keepdims=True)
    lane = lax.broadcasted_iota(I32, sq.shape, 1)
    lo = lane < gs
    s_lo = jnp.sum(jnp.where(lo, sq, 0.0), axis=-1, keepdims=True)
    s_hi = jnp.sum(jnp.where(lo, 0.0, sq), axis=-1, keepdims=True)
    return jnp.where(lo, s_lo, s_hi) * (1.0 / gs)


def _adaln_kernel(c_ref, w_ref, b_ref, o_ref):
    c = c_ref[...]
    s = (c * jax.nn.sigmoid(c)).astype(BF16)
    o_ref[...] = jnp.dot(s, w_ref[...].astype(BF16), preferred_element_type=F32) + b_ref[...]


def _adaln(c_all, w_ada, b_ada):
    depth, d, n = w_ada.shape
    r = c_all.shape[0]
    tn = _tile(n, 1024)
    return pl.pallas_call(
        _adaln_kernel,
        grid=(depth, n // tn),
        in_specs=[pl.BlockSpec((r, d), lambda l, j: (0, 0)),
                  pl.BlockSpec((None, d, tn), lambda l, j: (l, 0, j)),
                  pl.BlockSpec((None, 1, tn), lambda l, j: (l, 0, j))],
        out_specs=pl.BlockSpec((None, r, tn), lambda l, j: (l, 0, j)),
        out_shape=jax.ShapeDtypeStruct((depth, r, n), F32),
        compiler_params=_params(2),
        name="adaln",
    )(c_all, w_ada, b_ada.reshape(depth, 1, n))


def _norm_mod(x, g, sc, sh):
    xf = x
    y = xf * lax.rsqrt(jnp.mean(xf * xf, axis=-1, keepdims=True) + EPS) * g
    return y * (1.0 + sc) + sh


def _norm1_kernel(x_ref, g_ref, sc_ref, sh_ref, o_ref):
    o_ref[...] = _norm_mod(x_ref[...], g_ref[...], sc_ref[...], sh_ref[...]).astype(o_ref.dtype)


def _norm1(x, g, sc, sh):
    b, s, d = x.shape
    tm = _tile(s, 512)
    mod_spec = pl.BlockSpec((None, 1, d), lambda bi, i: (bi, 0, 0))
    return pl.pallas_call(
        _norm1_kernel,
        grid=(b, s // tm),
        in_specs=[pl.BlockSpec((None, tm, d), lambda bi, i: (bi, i, 0)),
                  pl.BlockSpec((1, d), lambda bi, i: (0, 0)), mod_spec, mod_spec],
        out_specs=pl.BlockSpec((None, tm, d), lambda bi, i: (bi, i, 0)),
        out_shape=jax.ShapeDtypeStruct((b, s, d), BF16),
        compiler_params=_params(2),
        name="norm1",
    )(x, g.reshape(1, d), sc, sh)


def _norm2_router_kernel(x_ref, g_ref, sc_ref, sh_ref, rw_ref, o_ref, lg_ref):
    h = _norm_mod(x_ref[...], g_ref[...], sc_ref[...], sh_ref[...])
    o_ref[...] = h
    lg_ref[...] = _dot3(h, rw_ref[...], (((1,), (0,)), ((), ())))


def _norm2_router(x, g, sc, sh, router_w):
    b, s, d = x.shape
    e = router_w.shape[-1]
    tm = _tile(s, 256)
    mod_spec = pl.BlockSpec((None, 1, d), lambda bi, i: (bi, 0, 0))
    return pl.pallas_call(
        _norm2_router_kernel,
        grid=(b, s // tm),
        in_specs=[pl.BlockSpec((None, tm, d), lambda bi, i: (bi, i, 0)),
                  pl.BlockSpec((1, d), lambda bi, i: (0, 0)), mod_spec, mod_spec,
                  pl.BlockSpec((d, e), lambda bi, i: (0, 0))],
        out_specs=[pl.BlockSpec((None, tm, d), lambda bi, i: (bi, i, 0)),
                   pl.BlockSpec((None, tm, e), lambda bi, i: (bi, i, 0))],
        out_shape=[jax.ShapeDtypeStruct((b, s, d), F32), jax.ShapeDtypeStruct((b, s, e), F32)],
        compiler_params=_params(2),
        name="norm2_router",
    )(x, g.reshape(1, d), sc, sh, router_w)


def _inproj_kernel(a_ref, w_ref, g_ref, o_ref, wb_ref, *, norm_ntiles, n_tiles, gs):
    j = pl.program_id(0)

    @pl.when((pl.program_id(1) == 0) & (pl.program_id(2) == 0))
    def _():
        wb_ref[...] = w_ref[...].astype(BF16)

    y = jnp.dot(a_ref[...], wb_ref[...], preferred_element_type=F32)

    def write_normed():
        g = g_ref[...]
        for c in range(y.shape[1] // LANES):
            sl = slice(c * LANES, (c + 1) * LANES)
            yc = y[:, sl]
            o_ref[:, sl] = (yc * lax.rsqrt(_group_ms(yc, gs) + EPS) * g[:, sl]).astype(o_ref.dtype)

    def write_raw():
        o_ref[...] = y.astype(o_ref.dtype)

    if norm_ntiles == 0:
        write_raw()
    elif norm_ntiles == n_tiles:
        write_normed()
    else:
        pl.when(j < norm_ntiles)(write_normed)
        pl.when(j >= norm_ntiles)(write_raw)


def _inproj(h, w_in, layer, col0, ncols, tn, out_dtype, gain=None, gs=LANES, norm_ntiles=0):
    b, s, k = h.shape
    tm = _tile(s, 512)
    n_tiles = ncols // tn
    cb = col0 // tn
    if gain is None:
        gain = jnp.ones((1, tn), F32)
    return pl.pallas_call(
        functools.partial(_inproj_kernel, norm_ntiles=norm_ntiles, n_tiles=n_tiles, gs=gs),
        grid=(n_tiles, b, s // tm),
        in_specs=[pl.BlockSpec((None, tm, k), lambda j, bi, i: (bi, i, 0)),
                  pl.BlockSpec((None, k, tn), lambda j, bi, i: (layer, 0, cb + j)),
                  pl.BlockSpec((1, tn), lambda j, bi, i: (0, 0))],
        out_specs=pl.BlockSpec((None, tm, tn), lambda j, bi, i: (bi, i, j)),
        out_shape=jax.ShapeDtypeStruct((b, s, ncols), out_dtype),
        scratch_shapes=[pltpu.VMEM((k, tn), BF16)],
        compiler_params=_params(3),
        name="inproj",
    )(h, w_in, gain)


def _kmean_kernel(k_ref, o_ref, *, nb):
    x = k_ref[...]
    o_ref[...] = jnp.mean(x.reshape(nb, MOBA_BLOCK, x.shape[-1]), axis=1)


def _kmean(kv, w):
    b, s, _ = kv.shape
    nf = s // MOBA_BLOCK
    nb = _tile(nf, 8)
    return pl.pallas_call(
        functools.partial(_kmean_kernel, nb=nb),
        grid=(b, nf // nb),
        in_specs=[pl.BlockSpec((None, nb * MOBA_BLOCK, w), lambda bi, i: (bi, i, 0))],
        out_specs=pl.BlockSpec((None, nb, w), lambda bi, i: (bi, i, 0)),
        out_shape=jax.ShapeDtypeStruct((b, nf, w), F32),
        compiler_params=_params(2),
        name="moba_kmean",
    )(kv)


def _select_topk_lowest_index(gate, idx, n_idx, k, axis):
    sel = jnp.zeros(gate.shape, jnp.bool_)
    for _ in range(k):
        m = jnp.max(gate, axis=axis, keepdims=True)
        first = jnp.min(jnp.where(gate == m, idx, n_idx), axis=axis, keepdims=True)
        pick = (idx == first) & (m > NEG_INF)
        sel = sel | pick
        gate = jnp.where(pick, NEG_INF, gate)
    return sel


def _moba_prompt_kernel(q_ref, k_ref, v_ref, km_ref, o_ref, *, scale):
    i = pl.program_id(2)
    q = q_ref[...]
    tq = q.shape[0]
    nf = km_ref.shape[0]
    gate = _dot3(q, km_ref[...], _NT)
    n_iota = lax.broadcasted_iota(I32, gate.shape, 1)
    gate = jnp.where(n_iota < i, gate, NEG_INF)
    sel = _select_topk_lowest_index(gate, n_iota, nf, MOBA_TOPK, 1)
    sel_f = sel.astype(F32)
    qb = q.astype(BF16)

    def attend(kb, vb, mask, carry):
        m_i, l_i, acc = carry
        s = lax.dot_general(qb, kb, _NT, preferred_element_type=F32) * scale
        s = jnp.where(mask, s, NEG_INF)
        m_new = jnp.maximum(m_i, jnp.max(s, axis=-1, keepdims=True))
        m_safe = jnp.where(m_new == NEG_INF, 0.0, m_new)
        p = jnp.exp(s - m_safe)
        alpha = jnp.exp(m_i - m_safe)
        l_new = alpha * l_i + jnp.sum(p, axis=-1, keepdims=True)
        acc = alpha * acc + jnp.dot(p.astype(BF16), vb, preferred_element_type=F32)
        return m_new, l_new, acc

    def body(n, carry):
        r0 = pl.multiple_of(n * MOBA_BLOCK, MOBA_BLOCK)
        kb = k_ref[pl.ds(r0, MOBA_BLOCK), :].astype(BF16)
        vb = v_ref[pl.ds(r0, MOBA_BLOCK), :].astype(BF16)
        chosen = jnp.sum(jnp.where(n_iota == n, sel_f, 0.0), axis=-1, keepdims=True) > 0.0
        return attend(kb, vb, chosen, carry)

    init = (jnp.full((tq, 1), NEG_INF, F32), jnp.zeros((tq, 1), F32), jnp.zeros((tq, HEAD_DIM), F32))
    carry = lax.fori_loop(0, i, body, init)
    r0 = pl.multiple_of(i * MOBA_BLOCK, MOBA_BLOCK)
    kb = k_ref[pl.ds(r0, MOBA_BLOCK), :].astype(BF16)
    vb = v_ref[pl.ds(r0, MOBA_BLOCK), :].astype(BF16)
    row = lax.broadcasted_iota(I32, (tq, MOBA_BLOCK), 0)
    col = lax.broadcasted_iota(I32, (tq, MOBA_BLOCK), 1)
    _, l_i, acc = attend(kb, vb, col <= row, carry)
    o_ref[...] = (acc / l_i).astype(o_ref.dtype)


def _moba_prompt(q, kv, kmean):
    b, s, w = q.shape
    nh = w // HEAD_DIM
    nq = s // MOBA_BLOCK
    nf = kmean.shape[1]
    return pl.pallas_call(
        functools.partial(_moba_prompt_kernel, scale=HEAD_DIM ** -0.5),
        grid=(b, nh, nq),
        in_specs=[pl.BlockSpec((None, MOBA_BLOCK, HEAD_DIM), lambda bi, h, i: (bi, i, h)),
                  pl.BlockSpec((None, s, HEAD_DIM), lambda bi, h, i: (bi, 0, h)),
                  pl.BlockSpec((None, s, HEAD_DIM), lambda bi, h, i: (bi, 0, nh + h)),
                  pl.BlockSpec((None, nf, HEAD_DIM), lambda bi, h, i: (bi, 0, h))],
        out_specs=pl.BlockSpec((None, MOBA_BLOCK, HEAD_DIM), lambda bi, h, i: (bi, i, h)),
        out_shape=jax.ShapeDtypeStruct((b, s, w), BF16),
        compiler_params=_params(3),
        name="moba_prompt",
    )(q, kv, kv, kmean)


def _diff_lambda(lq, lam_init):
    a = jnp.sum(lq[0:1] * lq[1:2], axis=-1, keepdims=True)
    b = jnp.sum(lq[2:3] * lq[3:4], axis=-1, keepdims=True)
    return jnp.exp(a) - jnp.exp(b) + lam_init


def _stack_maps(q, dd):
    lane = lax.broadcasted_iota(I32, q.shape, 1)
    return jnp.concatenate([jnp.where(lane < dd, q, 0.0), jnp.where(lane < dd, 0.0, q)], axis=0).astype(BF16)


def _diff_finish(acc, l_i, lam, g, lam_init, tq):
    o = acc / l_i
    d = o[:tq] - lam * o[tq:]
    ms = jnp.mean(d * d, axis=-1, keepdims=True)
    return d * lax.rsqrt(ms + EPS) * g * (1.0 - lam_init)


def _flash_step(qs, kb, vb, mask, carry, scale):
    m_i, l_i, acc = carry
    s = lax.dot_general(qs, kb, _NT, preferred_element_type=F32) * scale
    if mask is not None:
        s = jnp.where(mask, s, NEG_INF)
    m_new = jnp.maximum(m_i, jnp.max(s, axis=-1, keepdims=True))
    p = jnp.exp(s - m_new)
    alpha = jnp.exp(m_i - m_new)
    l_new = alpha * l_i + jnp.sum(p, axis=-1, keepdims=True)
    acc = alpha * acc + jnp.dot(p.astype(BF16), vb, preferred_element_type=F32)
    return m_new, l_new, acc


def _diff_prompt_kernel(q_ref, k_ref, v_ref, lq_ref, g_ref, o_ref, *, dd, lam_init, tk):
    i = pl.program_id(2)
    q = q_ref[...]
    tq = q.shape[0]
    scale = dd ** -0.5
    qs = _stack_maps(q, dd)

    def body(n, carry):
        r0 = pl.multiple_of(n * tk, tk)
        kb = k_ref[pl.ds(r0, tk), :].astype(BF16)
        vb = v_ref[pl.ds(r0, tk), :].astype(BF16)
        return _flash_step(qs, kb, vb, None, carry, scale)

    init = (jnp.full((2 * tq, 1), NEG_INF, F32), jnp.zeros((2 * tq, 1), F32),
            jnp.zeros((2 * tq, HEAD_DIM), F32))
    carry = lax.fori_loop(0, i, body, init)
    r0 = pl.multiple_of(i * tk, tk)
    kb = k_ref[pl.ds(r0, tk), :].astype(BF16)
    vb = v_ref[pl.ds(r0, tk), :].astype(BF16)
    row = lax.broadcasted_iota(I32, (2 * tq, tk), 0)
    col = lax.broadcasted_iota(I32, (2 * tq, tk), 1)
    row = jnp.where(row >= tq, row - tq, row)
    _, l_i, acc = _flash_step(qs, kb, vb, col <= row, carry, scale)
    lam = _diff_lambda(lq_ref[...], lam_init)
    o_ref[...] = _diff_finish(acc, l_i, lam, g_ref[...], lam_init, tq).astype(o_ref.dtype)


def _diff_prompt(q, kv, diff_lambda, subln_g, lam_init):
    b, s, w = q.shape
    nh = w // HEAD_DIM
    dd = diff_lambda.shape[-1]
    tq = _tile(s, 256)
    return pl.pallas_call(
        functools.partial(_diff_prompt_kernel, dd=dd, lam_init=lam_init, tk=tq),
        grid=(b, nh, s // tq),
        in_specs=[pl.BlockSpec((None, tq, HEAD_DIM), lambda bi, h, i: (bi, i, h)),
                  pl.BlockSpec((None, s, HEAD_DIM), lambda bi, h, i: (bi, 0, h)),
                  pl.BlockSpec((None, s, HEAD_DIM), lambda bi, h, i: (bi, 0, nh + h)),
                  pl.BlockSpec(diff_lambda.shape, lambda bi, h, i: (0, 0)),
                  pl.BlockSpec((1, HEAD_DIM), lambda bi, h, i: (0, 0))],
        out_specs=pl.BlockSpec((None, tq, HEAD_DIM), lambda bi, h, i: (bi, i, h)),
        out_shape=jax.ShapeDtypeStruct((b, s, w), BF16),
        compiler_params=_params(3),
        name="diff_prompt",
    )(q, kv, kv, diff_lambda, subln_g.reshape(1, HEAD_DIM))


def _paged_diff_kernel(pt_ref, q_ref, kvn_ref, lq_ref, g_ref, *rest, nh, pp, dd, lam_init):
    pages = rest[:pp]
    o_ref, m_ref, l_ref, acc_ref = rest[pp:]
    j = pl.program_id(1)
    w = nh * HEAD_DIM
    sq = q_ref.shape[0]
    scale = dd ** -0.5

    @pl.when(j == 0)
    def _():
        m_ref[...] = jnp.full(m_ref.shape, NEG_INF, F32)
        l_ref[...] = jnp.zeros(l_ref.shape, F32)
        acc_ref[...] = jnp.zeros(acc_ref.shape, F32)

    q = q_ref[...]
    for h in range(nh):
        hs = slice(h * HEAD_DIM, (h + 1) * HEAD_DIM)
        vs = slice(w + h * HEAD_DIM, w + (h + 1) * HEAD_DIM)
        qs = _stack_maps(q[:, hs], dd)
        s = jnp.concatenate(
            [lax.dot_general(qs, pg[:, hs].astype(BF16), _NT, preferred_element_type=F32) for pg in pages],
            axis=-1) * scale
        m_old = m_ref[h]
        m_new = jnp.maximum(m_old, jnp.max(s, axis=-1, keepdims=True))
        p = jnp.exp(s - m_new).astype(BF16)
        alpha = jnp.exp(m_old - m_new)
        ps = pages[0].shape[0]
        pv = sum(jnp.dot(p[:, r * ps:(r + 1) * ps], pages[r][:, vs].astype(BF16), preferred_element_type=F32)
                 for r in range(pp))
        l_ref[h] = alpha * l_ref[h] + jnp.sum(p.astype(F32), axis=-1, keepdims=True)
        acc_ref[h] = alpha * acc_ref[h] + pv
        m_ref[h] = m_new

    @pl.when(j == pl.num_programs(1) - 1)
    def _():
        lam = _diff_lambda(lq_ref[...], lam_init)
        row = lax.broadcasted_iota(I32, (2 * sq, sq), 0)
        col = lax.broadcasted_iota(I32, (2 * sq, sq), 1)
        row = jnp.where(row >= sq, row - sq, row)
        for h in range(nh):
            hs = slice(h * HEAD_DIM, (h + 1) * HEAD_DIM)
            vs = slice(w + h * HEAD_DIM, w + (h + 1) * HEAD_DIM)
            qs = _stack_maps(q[:, hs], dd)
            carry = (m_ref[h], l_ref[h], acc_ref[h])
            _, l_i, acc = _flash_step(qs, kvn_ref[:, hs].astype(BF16), kvn_ref[:, vs].astype(BF16),
                                      col <= row, carry, scale)
            o_ref[:, hs] = _diff_finish(acc, l_i, lam, g_ref[...], lam_init, sq).astype(o_ref.dtype)


def _page_specs(layer, pp, page, width):
    return [pl.BlockSpec((None, None, page, width),
                         functools.partial(lambda bi, j, pt, r: (layer, pt[bi, pp * j + r], 0, 0), r=r))
            for r in range(pp)]


def _paged_diff(q, kv_new, cache, layer, page_table, diff_lambda, subln_g, lam_init):
    db, sq, w = q.shape
    nh = w // HEAD_DIM
    dd = diff_lambda.shape[-1]
    n_pages = page_table.shape[1]
    page = cache.shape[2]
    pp = 4 if n_pages % 4 == 0 else 1
    specs = _page_specs(layer, pp, page, 2 * w)
    grid_spec = pltpu.PrefetchScalarGridSpec(
        num_scalar_prefetch=1,
        grid=(db, n_pages // pp),
        in_specs=[pl.BlockSpec((None, sq, w), lambda bi, j, pt: (bi, 0, 0)),
                  pl.BlockSpec((None, sq, 2 * w), lambda bi, j, pt: (bi, 0, 0)),
                  pl.BlockSpec(diff_lambda.shape, lambda bi, j, pt: (0, 0)),
                  pl.BlockSpec((1, HEAD_DIM), lambda bi, j, pt: (0, 0))] + specs,
        out_specs=pl.BlockSpec((None, sq, w), lambda bi, j, pt: (bi, 0, 0)),
        scratch_shapes=[pltpu.VMEM((nh, 2 * sq, 1), F32), pltpu.VMEM((nh, 2 * sq, 1), F32),
                        pltpu.VMEM((nh, 2 * sq, HEAD_DIM), F32)],
    )
    return pl.pallas_call(
        functools.partial(_paged_diff_kernel, nh=nh, pp=pp, dd=dd, lam_init=lam_init),
        grid_spec=grid_spec,
        out_shape=jax.ShapeDtypeStruct((db, sq, w), BF16),
        compiler_params=_params(2),
        name="diff_paged",
    )(page_table, q, kv_new, diff_lambda, subln_g.reshape(1, HEAD_DIM), *([cache] * pp))


def _paged_moba_kernel(pt_ref, q_ref, kvn_ref, *rest, nh, bps, scale):
    pages = rest[:2 * bps]
    o_ref, m_ref, l_ref, o_acc_ref, g_ref = rest[2 * bps:]
    j = pl.program_id(1)
    nb = m_ref.shape[0]
    w = nh * HEAD_DIM
    sq = q_ref.shape[0]
    ps = pages[0].shape[0]
    q = q_ref[...]

    for blk in range(bps):
        n = j * bps + blk
        p0, p1 = pages[2 * blk], pages[2 * blk + 1]
        for h in range(nh):
            hs = slice(h * HEAD_DIM, (h + 1) * HEAD_DIM)
            vs = slice(w + h * HEAD_DIM, w + (h + 1) * HEAD_DIM)
            qh = q[:, hs]
            qb = qh.astype(BF16)
            k0, k1 = p0[:, hs], p1[:, hs]
            kmean = (jnp.sum(k0, axis=0, keepdims=True) + jnp.sum(k1, axis=0, keepdims=True)) * (1.0 / (2 * ps))
            g_ref[n, h] = jnp.sum(qh * kmean, axis=-1, keepdims=True)
            s = jnp.concatenate(
                [lax.dot_general(qb, k0.astype(BF16), _NT, preferred_element_type=F32),
                 lax.dot_general(qb, k1.astype(BF16), _NT, preferred_element_type=F32)], axis=-1) * scale
            m = jnp.max(s, axis=-1, keepdims=True)
            p = jnp.exp(s - m).astype(BF16)
            m_ref[n, h] = m
            l_ref[n, h] = jnp.sum(p.astype(F32), axis=-1, keepdims=True)
            o_acc_ref[n, h] = (jnp.dot(p[:, :ps], p0[:, vs].astype(BF16), preferred_element_type=F32)
                               + jnp.dot(p[:, ps:], p1[:, vs].astype(BF16), preferred_element_type=F32))

    @pl.when(j == pl.num_programs(1) - 1)
    def _():
        row = lax.broadcasted_iota(I32, (sq, sq), 0)
        col = lax.broadcasted_iota(I32, (sq, sq), 1)
        n_iota = lax.broadcasted_iota(I32, (nb, sq, 1), 0)
        for h in range(nh):
            hs = slice(h * HEAD_DIM, (h + 1) * HEAD_DIM)
            vs = slice(w + h * HEAD_DIM, w + (h + 1) * HEAD_DIM)
            gate = g_ref[:, h]
            sel = _select_topk_lowest_index(gate, n_iota, nb, MOBA_TOPK, 0)
            m_blk = jnp.where(sel, m_ref[:, h], NEG_INF)
            qb = q[:, hs].astype(BF16)
            s_own = lax.dot_general(qb, kvn_ref[:, hs].astype(BF16), _NT, preferred_element_type=F32) * scale
            s_own = jnp.where(col <= row, s_own, NEG_INF)
            m_own = jnp.max(s_own, axis=-1, keepdims=True)
            m_all = jnp.maximum(jnp.max(m_blk, axis=0), m_own)
            p_own = jnp.exp(s_own - m_all).astype(BF16)
            e_blk = jnp.where(sel, jnp.exp(m_blk - m_all[None]), 0.0)
            l_all = jnp.sum(e_blk * l_ref[:, h], axis=0) + jnp.sum(p_own.astype(F32), axis=-1, keepdims=True)
            o_all = (jnp.sum(e_blk * o_acc_ref[:, h], axis=0)
                     + jnp.dot(p_own, kvn_ref[:, vs].astype(BF16), preferred_element_type=F32))
            o_ref[:, hs] = (o_all / l_all).astype(o_ref.dtype)


def _paged_moba(q, kv_new, cache, layer, page_table):
    db, sq, w = q.shape
    nh = w // HEAD_DIM
    n_pages = page_table.shape[1]
    page = cache.shape[2]
    assert 2 * page == MOBA_BLOCK and n_pages % 2 == 0 and sq <= MOBA_BLOCK
    nb = n_pages // 2
    bps = 2 if nb % 2 == 0 else 1
    pp = 2 * bps
    specs = _page_specs(layer, pp, page, 2 * w)
    grid_spec = pltpu.PrefetchScalarGridSpec(
        num_scalar_prefetch=1,
        grid=(db, nb // bps),
        in_specs=[pl.BlockSpec((None, sq, w), lambda bi, j, pt: (bi, 0, 0)),
                  pl.BlockSpec((None, sq, 2 * w), lambda bi, j, pt: (bi, 0, 0))] + specs,
        out_specs=pl.BlockSpec((None, sq, w), lambda bi, j, pt: (bi, 0, 0)),
        scratch_shapes=[pltpu.VMEM((nb, nh, sq, 1), F32), pltpu.VMEM((nb, nh, sq, 1), F32),
                        pltpu.VMEM((nb, nh, sq, HEAD_DIM), F32), pltpu.VMEM((nb, nh, sq, 1), F32)],
    )
    return pl.pallas_call(
        functools.partial(_paged_moba_kernel, nh=nh, bps=bps, scale=HEAD_DIM ** -0.5),
        grid_spec=grid_spec,
        out_shape=jax.ShapeDtypeStruct((db, sq, w), BF16),
        compiler_params=_params(2),
        name="moba_paged",
    )(page_table, q, kv_new, *([cache] * pp))


def _shift_rows(x, d, fill):
    row = lax.broadcasted_iota(I32, x.shape, 0)
    return jnp.where(row >= d, pltpu.roll(x, d, 0), fill)


def _rglru_kernel(x_ref, h0_ref, c0_ref, cw_ref, cb_ref, wa_ref, ba_ref, wx_ref, bx_ref, lam_ref,
                  o_ref, hl_ref, cl_ref, hc_ref, tail_ref, *, cw_len):
    c = pl.program_id(2)
    nc = pl.num_programs(2)
    x = x_ref[...]
    tc = x.shape[0]
    nt = cw_len - 1

    @pl.when(c == 0)
    def _():
        hc_ref[...] = h0_ref[...]
        tail_ref[...] = c0_ref[...]

    tail = tail_ref[...]
    row = lax.broadcasted_iota(I32, x.shape, 0)
    cw = cw_ref[...]
    u = cb_ref[...] + x * cw[nt:nt + 1]
    for d in range(1, cw_len):
        fill = jnp.zeros_like(x)
        for r in range(d):
            fill = jnp.where(row == r, tail[nt - d + r:nt - d + r + 1], fill)
        u = u + _shift_rows(x, d, fill) * cw[nt - d:nt - d + 1]

    ub = u.astype(BF16)
    r_g = jax.nn.sigmoid(jnp.dot(ub, wa_ref[...].astype(BF16), preferred_element_type=F32) + ba_ref[...])
    i_g = jax.nn.sigmoid(jnp.dot(ub, wx_ref[...].astype(BF16), preferred_element_type=F32) + bx_ref[...])
    lam = lam_ref[...]
    log_sig = jnp.minimum(lam, 0.0) - jnp.log1p(jnp.exp(-jnp.abs(lam)))
    log_a = RGLRU_C * r_g * log_sig
    a = jnp.exp(log_a)
    bv = jnp.sqrt(-jnp.tanh(log_a) * (a * a + 1.0)) * (i_g * u)

    d = 1
    while d < tc:
        a_prev = _shift_rows(a, d, jnp.ones_like(a))
        b_prev = _shift_rows(bv, d, jnp.zeros_like(bv))
        bv = a * b_prev + bv
        a = a * a_prev
        d *= 2
    h = a * hc_ref[...] + bv
    o_ref[...] = h.astype(o_ref.dtype)
    hc_ref[...] = h[tc - 1:tc]
    if tc >= nt:
        tail_ref[...] = x[tc - nt:tc]
    else:
        tail_ref[...] = jnp.concatenate([tail[tc:], x], axis=0)

    @pl.when(c == nc - 1)
    def _():
        hl_ref[...] = hc_ref[...]
        cl_ref[...] = tail_ref[...]


def _rglru(xr, h0, conv0, conv_w, conv_b, w_a, b_a, w_x, b_x, lam):
    b, s, w = xr.shape
    nblk, bw, _ = w_a.shape
    assert bw == LANES
    cw_len = conv_w.shape[0]
    tc = _tile(s, 256)
    vec = lambda a: a.reshape(1, w)
    vspec = pl.BlockSpec((1, bw), lambda bi, n, c: (0, n))
    wspec = pl.BlockSpec((None, bw, bw), lambda bi, n, c: (n, 0, 0))
    return pl.pallas_call(
        functools.partial(_rglru_kernel, cw_len=cw_len),
        grid=(b, nblk, s // tc),
        in_specs=[pl.BlockSpec((None, tc, bw), lambda bi, n, c: (bi, c, n)),
                  pl.BlockSpec((None, 1, bw), lambda bi, n, c: (bi, 0, n)),
                  pl.BlockSpec((None, cw_len - 1, bw), lambda bi, n, c: (bi, 0, n)),
                  pl.BlockSpec((cw_len, bw), lambda bi, n, c: (0, n)),
                  vspec, wspec, vspec, wspec, vspec, vspec],
        out_specs=[pl.BlockSpec((None, tc, bw), lambda bi, n, c: (bi, c, n)),
                   pl.BlockSpec((None, 1, bw), lambda bi, n, c: (bi, 0, n)),
                   pl.BlockSpec((None, cw_len - 1, bw), lambda bi, n, c: (bi, 0, n))],
        out_shape=[jax.ShapeDtypeStruct((b, s, w), BF16),
                   jax.ShapeDtypeStruct((b, 1, w), F32),
                   jax.ShapeDtypeStruct((b, cw_len - 1, w), F32)],
        scratch_shapes=[pltpu.VMEM((1, bw), F32), pltpu.VMEM((cw_len - 1, bw), F32)],
        compiler_params=_params(3),
        name="rglru",
    )(xr, h0.reshape(b, 1, w), conv0, conv_w, vec(conv_b), w_a, vec(b_a), w_x, vec(b_x), vec(lam))


def _merge_kernel(a0_ref, a1_ref, a2_ref, g0_ref, g1_ref, g2_ref, w_ref, o_ref, wb_ref):
    @pl.when((pl.program_id(1) == 0) & (pl.program_id(2) == 0))
    def _():
        wb_ref[...] = w_ref[...].astype(BF16)

    acc = None
    for n, (a_ref, g_ref) in enumerate(((a0_ref, g0_ref), (a1_ref, g1_ref), (a2_ref, g2_ref))):
        y = jnp.dot(a_ref[...], wb_ref[n], preferred_element_type=F32) * jax.nn.sigmoid(g_ref[...])
        acc = y if acc is None else acc + y
    o_ref[...] = acc.astype(o_ref.dtype)


def _merge(oa, od, orr, gl, w_branch, layer):
    b, s, bw = oa.shape
    d = w_branch.shape[-1]
    tm = _tile(s, 512)
    tn = _tile(d, 512)
    nt = d // tn
    a_spec = pl.BlockSpec((None, tm, bw), lambda j, bi, i: (bi, i, 0))
    g_specs = [pl.BlockSpec((None, tm, tn), functools.partial(lambda j, bi, i, n: (bi, i, n * nt + j), n=n))
               for n in range(3)]
    return pl.pallas_call(
        _merge_kernel,
        grid=(nt, b, s // tm),
        in_specs=[a_spec, a_spec, a_spec] + g_specs
                 + [pl.BlockSpec((None, 3, bw, tn), lambda j, bi, i: (layer, 0, 0, j))],
        out_specs=pl.BlockSpec((None, tm, tn), lambda j, bi, i: (bi, i, j)),
        out_shape=jax.ShapeDtypeStruct((b, s, d), BF16),
        scratch_shapes=[pltpu.VMEM((3, bw, tn), BF16)],
        compiler_params=_params(3),
        name="branch_merge",
    )(oa, od, orr, gl, gl, gl, w_branch)


def _outproj_kernel(a_ref, w_ref, x_ref, g_ref, o_ref, wb_ref):
    @pl.when((pl.program_id(1) == 0) & (pl.program_id(2) == 0))
    def _():
        wb_ref[...] = w_ref[...].astype(BF16)

    y = jnp.dot(a_ref[...], wb_ref[...], preferred_element_type=F32)
    o_ref[...] = x_ref[...] + g_ref[...] * y


def _outproj(merged, w_o, layer, x, g1):
    b, s, d = x.shape
    tm = _tile(s, 512)
    tn = _tile(d, 1024)
    return pl.pallas_call(
        _outproj_kernel,
        grid=(d // tn, b, s // tm),
        in_specs=[pl.BlockSpec((None, tm, d), lambda j, bi, i: (bi, i, 0)),
                  pl.BlockSpec((None, d, tn), lambda j, bi, i: (layer, 0, j)),
                  pl.BlockSpec((None, tm, tn), lambda j, bi, i: (bi, i, j)),
                  pl.BlockSpec((None, 1, tn), lambda j, bi, i: (bi, 0, j))],
        out_specs=pl.BlockSpec((None, tm, tn), lambda j, bi, i: (bi, i, j)),
        out_shape=jax.ShapeDtypeStruct((b, s, d), F32),
        scratch_shapes=[pltpu.VMEM((d, tn), BF16)],
        compiler_params=_params(3),
        name="outproj",
    )(merged, w_o, x, g1)


def _stack_rows(rows):
    n = rows[0].shape[1]
    r_iota = lax.broadcasted_iota(I32, (len(rows), n), 0)
    out = jnp.zeros((len(rows), n), rows[0].dtype)
    for k, row in enumerate(rows):
        out = jnp.where(r_iota == k, row, out)
    return out


def _route_kernel(lg_ref, bias_ref, e_ref, w_ref, rank_ref, cnt_ref, carry_ref, *, n_exp):
    i = pl.program_id(0)

    @pl.when(i == 0)
    def _():
        carry_ref[...] = jnp.zeros(carry_ref.shape, F32)

    tile = lg_ref.shape[1]
    pg = n_exp // N_GROUPS
    scores = jax.nn.sigmoid(lg_ref[...])
    biased = scores + bias_ref[...]
    sc3 = scores.reshape(N_GROUPS, pg, tile)
    b3 = biased.reshape(N_GROUPS, pg, tile)
    j_iota = lax.broadcasted_iota(I32, b3.shape, 1)
    eid = lax.broadcasted_iota(I32, b3.shape, 0) * pg + j_iota

    m1 = jnp.max(b3, axis=1, keepdims=True)
    f1 = jnp.min(jnp.where(b3 == m1, j_iota, pg), axis=1, keepdims=True)
    m2 = jnp.max(jnp.where(j_iota == f1, NEG_INF, b3), axis=1, keepdims=True)
    grp = m1 + m2
    g_iota = lax.broadcasted_iota(I32, grp.shape, 0)
    g_sel = _select_topk_lowest_index(grp, g_iota, N_GROUPS, TOPK_GROUPS, 0)
    work = jnp.where(g_sel, b3, NEG_INF)

    e_rows, w_rows = [], []
    multi_hot = jnp.zeros(b3.shape, F32)
    for _ in range(TOP_K):
        m = jnp.max(jnp.max(work, axis=1, keepdims=True), axis=0, keepdims=True)
        first = jnp.min(jnp.min(jnp.where(work == m, eid, n_exp), axis=1, keepdims=True), axis=0, keepdims=True)
        pick = eid == first
        w_k = jnp.sum(jnp.sum(jnp.where(pick, sc3, 0.0), axis=1, keepdims=True), axis=0, keepdims=True)
        work = jnp.where(pick, NEG_INF, work)
        multi_hot = multi_hot + pick.astype(F32)
        e_rows.append(first.reshape(1, tile))
        w_rows.append(w_k.reshape(1, tile))
    e_idx = _stack_rows(e_rows)
    w = _stack_rows(w_rows)
    w = w / jnp.sum(w, axis=0, keepdims=True) * ROUTED_SCALE

    mh = multi_hot.reshape(n_exp, tile)
    r_i = lax.broadcasted_iota(I32, (tile, tile), 0)
    c_i = lax.broadcasted_iota(I32, (tile, tile), 1)
    before = (r_i < c_i).astype(BF16)
    base = carry_ref[:, 0:1] + jnp.dot(mh.astype(BF16), before, preferred_element_type=F32)
    e_sub = lax.broadcasted_iota(I32, (n_exp, tile), 0)
    ranks = [jnp.sum(jnp.where(e_sub == e_rows[k], base, 0.0), axis=0, keepdims=True) for k in range(TOP_K)]
    e_ref[...] = e_idx
    w_ref[...] = w
    rank_ref[...] = _stack_rows(ranks).astype(I32)
    carry_ref[...] = carry_ref[...] + jnp.sum(mh, axis=1, keepdims=True)
    cnt_ref[...] = carry_ref[...]


def _route(logits_t, bias):
    n_exp, t = logits_t.shape
    tile = 256
    out_spec = pl.BlockSpec((TOP_K, tile), lambda i: (0, i))
    return pl.pallas_call(
        functools.partial(_route_kernel, n_exp=n_exp),
        grid=(t // tile,),
        in_specs=[pl.BlockSpec((n_exp, tile), lambda i: (0, i)),
                  pl.BlockSpec((n_exp, 1), lambda i: (0, 0))],
        out_specs=[out_spec, out_spec, out_spec, pl.BlockSpec((n_exp, LANES), lambda i: (0, 0))],
        out_shape=[jax.ShapeDtypeStruct((TOP_K, t), I32), jax.ShapeDtypeStruct((TOP_K, t), F32),
                   jax.ShapeDtypeStruct((TOP_K, t), I32), jax.ShapeDtypeStruct((n_exp, LANES), F32)],
        scratch_shapes=[pltpu.VMEM((n_exp, LANES), F32)],
        compiler_params=_params(1),
        name="route",
    )(logits_t, bias.reshape(n_exp, 1))


def _slot_kernel(e_ref, rank_ref, ps_ref, o_ref):
    e_idx = e_ref[...]
    n_exp = ps_ref.shape[0]
    tile = e_idx.shape[1]
    e_sub = lax.broadcasted_iota(I32, (n_exp, tile), 0)
    ps = ps_ref[...]
    rows = [jnp.sum(jnp.where(e_sub == e_idx[k:k + 1], ps, 0), axis=0, keepdims=True) for k in range(TOP_K)]
    o_ref[...] = _stack_rows(rows) + rank_ref[...]


def _slots(e_idx, rank, p_start):
    _, t = e_idx.shape
    n_exp = p_start.shape[0]
    tile = 256
    spec = pl.BlockSpec((TOP_K, tile), lambda i: (0, i))
    return pl.pallas_call(
        _slot_kernel,
        grid=(t // tile,),
        in_specs=[spec, spec, pl.BlockSpec((n_exp, 1), lambda i: (0, 0))],
        out_specs=spec,
        out_shape=jax.ShapeDtypeStruct((TOP_K, t), I32),
        compiler_params=_params(1),
        name="slots",
    )(e_idx, rank, p_start.reshape(n_exp, 1))


def _zero_tails_kernel(blk_ref, o_ref):
    o_ref[...] = jnp.zeros(o_ref.shape, o_ref.dtype)


def _zero_tails(last_block, n_slots, d, tm):
    n_exp = last_block.shape[0]
    grid_spec = pltpu.PrefetchScalarGridSpec(
        num_scalar_prefetch=1, grid=(n_exp,), in_specs=[],
        out_specs=pl.BlockSpec((tm, d), lambda e, blk: (blk[e], 0)))
    return pl.pallas_call(
        _zero_tails_kernel, grid_spec=grid_spec,
        out_shape=jax.ShapeDtypeStruct((n_slots, d), F32),
        compiler_params=_params(1),
        name="zero_tails",
    )(last_block)


def _row_copy(src_ref, src_row, dst_ref, dst_row, sem):
    return pltpu.make_async_copy(src_ref.at[pl.ds(src_row, 1), :], dst_ref.at[pl.ds(dst_row, 1), :], sem)


def _dispatch_kernel(pos_ref, h_ref, xs_in_ref, xs_ref, sem):
    del xs_in_ref
    tile = h_ref.shape[0]

    def issue(t, _):
        for k in range(TOP_K):
            _row_copy(h_ref, t, xs_ref, pos_ref[k, t], sem).start()
        return 0

    lax.fori_loop(0, tile, issue, 0)

    def drain(t, _):
        for k in range(TOP_K):
            _row_copy(h_ref, 0, xs_ref, 0, sem).wait()
        return 0

    lax.fori_loop(0, tile, drain, 0)


def _dispatch(pos_t, h_all, xs_init):
    t, d = h_all.shape
    tile = 256
    return pl.pallas_call(
        _dispatch_kernel,
        grid=(t // tile,),
        in_specs=[pl.BlockSpec((TOP_K, tile), lambda i: (0, i), memory_space=pltpu.SMEM),
                  pl.BlockSpec((tile, d), lambda i: (i, 0)),
                  pl.BlockSpec(memory_space=pl.ANY)],
        out_specs=pl.BlockSpec(memory_space=pl.ANY),
        out_shape=jax.ShapeDtypeStruct(xs_init.shape, xs_init.dtype),
        scratch_shapes=[pltpu.SemaphoreType.DMA],
        input_output_aliases={2: 0},
        compiler_params=_params(1),
        name="dispatch",
    )(pos_t, h_all, xs_init)


def _experts_kernel(be_ref, nu_ref, x_ref, wg_ref, wu_ref, wd_ref, o_ref, wgb_ref, wub_ref, wdb_ref):
    b = pl.program_id(0)
    prev = be_ref[jnp.maximum(b - 1, 0)]
    new_expert = (b == 0) | (be_ref[b] != prev)

    @pl.when(new_expert & (b < nu_ref[0]))
    def _():
        wgb_ref[...] = wg_ref[...].astype(BF16)
        wub_ref[...] = wu_ref[...].astype(BF16)
        wdb_ref[...] = wd_ref[...].astype(BF16)

    @pl.when(b < nu_ref[0])
    def _():
        x = x_ref[...].astype(BF16)
        g = jnp.dot(x, wgb_ref[...], preferred_element_type=F32)
        u = jnp.dot(x, wub_ref[...], preferred_element_type=F32)
        mid = (g * jax.nn.sigmoid(g) * u).astype(BF16)
        o_ref[...] = jnp.dot(mid, wdb_ref[...], preferred_element_type=F32)


def _experts(block_e, n_used, xs, w_gate, w_up, w_down, layer, tm):
    n_slots, d = xs.shape
    de = w_gate.shape[-1]
    n_blocks = n_slots // tm

    def row_map(b, be, nu):
        return (jnp.minimum(b, nu[0] - 1), 0)

    grid_spec = pltpu.PrefetchScalarGridSpec(
        num_scalar_prefetch=2,
        grid=(n_blocks,),
        in_specs=[pl.BlockSpec((tm, d), row_map),
                  pl.BlockSpec((None, None, d, de), lambda b, be, nu: (layer, be[b], 0, 0)),
                  pl.BlockSpec((None, None, d, de), lambda b, be, nu: (layer, be[b], 0, 0)),
                  pl.BlockSpec((None, None, de, d), lambda b, be, nu: (layer, be[b], 0, 0))],
        out_specs=pl.BlockSpec((tm, d), row_map),
        scratch_shapes=[pltpu.VMEM((d, de), BF16), pltpu.VMEM((d, de), BF16), pltpu.VMEM((de, d), BF16)],
    )
    return pl.pallas_call(
        _experts_kernel, grid_spec=grid_spec,
        out_shape=jax.ShapeDtypeStruct((n_slots, d), F32),
        compiler_params=_params(1),
        name="experts",
    )(block_e, n_used, xs, w_gate, w_up, w_down)


def _combine_kernel(pos_ref, h_ref, x_ref, g_ref, w_ref, sg_ref, su_ref, sd_ref, ys_ref, o_ref,
                    buf_ref, sgb_ref, sub_ref, sdb_ref, sem):
    tile = h_ref.shape[0]

    @pl.when(pl.program_id(0) == 0)
    def _():
        sgb_ref[...] = sg_ref[...].astype(BF16)
        sub_ref[...] = su_ref[...].astype(BF16)
        sdb_ref[...] = sd_ref[...].astype(BF16)

    def issue(t, _):
        for k in range(TOP_K):
            _row_copy(ys_ref, pos_ref[k, t], buf_ref.at[k], t, sem).start()
        return 0

    lax.fori_loop(0, tile, issue, 0)

    hb = h_ref[...].astype(BF16)
    g = jnp.dot(hb, sgb_ref[...], preferred_element_type=F32)
    u = jnp.dot(hb, sub_ref[...], preferred_element_type=F32)
    mid = (g * jax.nn.sigmoid(g) * u).astype(BF16)
    acc = jnp.dot(mid, sdb_ref[...], preferred_element_type=F32)

    def drain(t, _):
        for k in range(TOP_K):
            _row_copy(ys_ref, 0, buf_ref.at[k], 0, sem).wait()
        return 0

    lax.fori_loop(0, tile, drain, 0)

    w = w_ref[...]
    for k in range(TOP_K):
        acc = acc + w[:, k:k + 1] * buf_ref[k]
    o_ref[...] = x_ref[...] + g_ref[...] * acc


def _combine(pos_t, h_all, row0, x, g2, w_tok, ys, s_gate, s_up, s_down, layer):
    t, d = x.shape
    de = s_gate.shape[-1]
    tile = 128
    b0 = row0 // tile
    g_rows = g2.shape[1]
    return pl.pallas_call(
        _combine_kernel,
        grid=(t // tile,),
        in_specs=[pl.BlockSpec((TOP_K, tile), lambda i: (0, b0 + i), memory_space=pltpu.SMEM),
                  pl.BlockSpec((tile, d), lambda i: (b0 + i, 0)),
                  pl.BlockSpec((tile, d), lambda i: (i, 0)),
                  pl.BlockSpec((None, g_rows, d), lambda i: (i, 0, 0)),
                  pl.BlockSpec((tile, TOP_K), lambda i: (b0 + i, 0)),
                  pl.BlockSpec((None, d, de), lambda i: (layer, 0, 0)),
                  pl.BlockSpec((None, d, de), lambda i: (layer, 0, 0)),
                  pl.BlockSpec((None, de, d), lambda i: (layer, 0, 0)),
                  pl.BlockSpec(memory_space=pl.ANY)],
        out_specs=pl.BlockSpec((tile, d), lambda i: (i, 0)),
        out_shape=jax.ShapeDtypeStruct((t, d), F32),
        scratch_shapes=[pltpu.VMEM((TOP_K, tile, d), F32), pltpu.VMEM((d, de), BF16),
                        pltpu.VMEM((d, de), BF16), pltpu.VMEM((de, d), BF16), pltpu.SemaphoreType.DMA],
        compiler_params=_params(1),
        name="combine",
    )(pos_t, h_all, x, g2, w_tok, s_gate, s_up, s_down, ys)


def _mixers(x, mods, layer, p, lam_init, past):
    b, s, d = x.shape
    w = p["moba_q_norm_g"].shape[-1] * (d // 256)
    sh1, sc1, g1 = mods
    h = _norm1(x, p["norm1_g"][layer], sc1, sh1)
    nh = w // HEAD_DIM
    tile_gain = lambda g: jnp.tile(g.reshape(1, -1), (1, nh))
    w_in = p["w_in"]
    tn = w
    qa = _inproj(h, w_in, layer, 0, w, tn, F32, tile_gain(p["moba_q_norm_g"][layer]), LANES, 1)
    kv_a = _inproj(h, w_in, layer, w, 2 * w, tn, F32, tile_gain(p["moba_k_norm_g"][layer]), LANES, 1)
    dd = p["diff_dim"]
    qd = _inproj(h, w_in, layer, 3 * w, w, tn, F32, tile_gain(p["diff_q_norm_g"][layer]), dd, 1)
    kv_d = _inproj(h, w_in, layer, 4 * w, 2 * w, tn, F32, tile_gain(p["diff_k_norm_g"][layer]), dd, 1)
    xr = _inproj(h, w_in, layer, 6 * w, w, tn, F32)
    gl = _inproj(h, w_in, layer, 7 * w, 3 * d, tn, F32)

    if past is None:
        oa = _moba_prompt(qa, kv_a, _kmean(kv_a, w))
        od = _diff_prompt(qd, kv_d, p["diff_lambda"][layer], p["diff_subln_g"][layer], lam_init)
        h0 = jnp.zeros((b, w), F32)
        conv0 = jnp.zeros((b, p["rglru_conv_w"].shape[1] - 1, w), F32)
    else:
        cache_a, cache_d, page_table, h0, conv0 = past
        oa = _paged_moba(qa, kv_a, cache_a, layer, page_table)
        od = _paged_diff(qd, kv_d, cache_d, layer, page_table, p["diff_lambda"][layer],
                         p["diff_subln_g"][layer], lam_init)
    orr, h_last, conv_last = _rglru(xr, h0, conv0, p["rglru_conv_w"][layer], p["rglru_conv_b"][layer],
                                    p["rglru_wa"][layer], p["rglru_ba"][layer], p["rglru_wx"][layer],
                                    p["rglru_bx"][layer], p["rglru_lambda"][layer])
    merged = _merge(oa, od, orr, gl, p["w_branch"], layer)
    x1 = _outproj(merged, p["w_o"], layer, x, g1)
    return x1, kv_a, kv_d, h_last.reshape(b, w), conv_last


def _moe(x1p, x1s, mods_p, mods_s, layer, p):
    bp, sp, d = x1p.shape
    bs, ss, _ = x1s.shape
    n_exp = p["router_w"].shape[-1]
    tp, ts = bp * sp, bs * ss
    h2p, lgp = _norm2_router(x1p, p["norm2_g"][layer], mods_p[1], mods_p[0], p["router_w"][layer])
    h2s, lgs = _norm2_router(x1s, p["norm2_g"][layer], mods_s[1], mods_s[0], p["router_w"][layer])
    t_all = tp + -(-ts // 256) * 256
    pad = t_all - tp - ts
    h_all = jnp.concatenate([h2p.reshape(tp, d), h2s.reshape(ts, d), jnp.zeros((pad, d), F32)], axis=0)
    lg_all = jnp.concatenate([lgp.reshape(tp, n_exp), lgs.reshape(ts, n_exp), jnp.zeros((pad, n_exp), F32)], axis=0)
    e_idx, w_t, rank, cnt = _route(lg_all.T, p["router_bias"][layer])

    tm = 256
    n_blocks = -(-(t_all * TOP_K) // tm) + n_exp
    counts = cnt[:, 0].astype(I32)
    padded = (counts + tm - 1) // tm * tm
    p_end = jnp.cumsum(padded)
    p_start = p_end - padded
    n_used = (p_end[-1:] // tm).astype(I32)
    block_e = jnp.minimum(jnp.searchsorted(p_end, jnp.arange(n_blocks, dtype=I32) * tm, side="right"),
                          n_exp - 1).astype(I32)
    last_block = jnp.maximum(p_end // tm - 1, 0).astype(I32)
    pos_t = _slots(e_idx, rank, p_start.astype(I32))

    xs = _dispatch(pos_t, h_all, _zero_tails(last_block, n_blocks * tm, d, tm))
    ys = _experts(block_e, n_used, xs, p["expert_w_gate"], p["expert_w_up"], p["expert_w_down"], layer, tm)

    w_tok = w_t.T
    tile = 128
    shared = (p["shared_w_gate"], p["shared_w_up"], p["shared_w_down"], layer)
    g2p = jnp.broadcast_to(mods_p[2][:, None], (bp, sp // tile, 1, d)).reshape(tp // tile, 1, d)
    yp = _combine(pos_t, h_all, 0, x1p.reshape(tp, d), g2p, w_tok, ys, *shared)
    ts_pad = -(-ts // tile) * tile
    x1s_pad = jnp.concatenate([x1s.reshape(ts, d), jnp.zeros((ts_pad - ts, d), F32)], axis=0)
    g2s = jnp.broadcast_to(mods_s[2], (bs, ss, d)).reshape(ts, d)
    g2s = jnp.concatenate([g2s, jnp.zeros((ts_pad - ts, d), F32)], axis=0).reshape(ts_pad // tile, tile, d)
    ysamp = _combine(pos_t, h_all, tp, x1s_pad, g2s, w_tok, ys, *shared)
    return yp.reshape(bp, sp, d), ysamp[:ts].reshape(bs, ss, d)


def kernel(x_prompt, x_sample, cache_moba_kv, cache_diff_kv, state_rglru_h, state_rglru_conv, page_table, c_prompt, c_sample, w_ada, b_ada, norm1_g, norm2_g, w_in, moba_q_norm_g, moba_k_norm_g, diff_q_norm_g, diff_k_norm_g, diff_lambda, diff_subln_g, rglru_conv_w, rglru_conv_b, rglru_wa, rglru_ba, rglru_wx, rglru_bx, rglru_lambda, w_branch, w_o, router_w, router_bias, expert_w_gate, expert_w_up, expert_w_down, shared_w_gate, shared_w_up, shared_w_down):
    depth = w_ada.shape[0]
    bp, sp, d = x_prompt.shape
    bs, ss, _ = x_sample.shape
    nh, hd = cache_moba_kv.shape[-2:]
    assert hd == HEAD_DIM and sp % MOBA_BLOCK == 0 and (bp * sp) % 256 == 0
    p = dict(norm1_g=norm1_g, norm2_g=norm2_g, w_in=w_in, moba_q_norm_g=moba_q_norm_g,
             moba_k_norm_g=moba_k_norm_g, diff_dim=diff_q_norm_g.shape[-1],
             diff_q_norm_g=diff_q_norm_g.reshape(depth, -1), diff_k_norm_g=diff_k_norm_g.reshape(depth, -1),
             diff_lambda=diff_lambda, diff_subln_g=diff_subln_g, rglru_conv_w=rglru_conv_w,
             rglru_conv_b=rglru_conv_b, rglru_wa=rglru_wa, rglru_ba=rglru_ba, rglru_wx=rglru_wx,
             rglru_bx=rglru_bx, rglru_lambda=rglru_lambda, w_branch=w_branch, w_o=w_o,
             router_w=router_w, router_bias=router_bias, expert_w_gate=expert_w_gate,
             expert_w_up=expert_w_up, expert_w_down=expert_w_down, shared_w_gate=shared_w_gate,
             shared_w_up=shared_w_up, shared_w_down=shared_w_down)
    n_c = bp + bs
    r = -(-n_c // 8) * 8
    c_all = jnp.concatenate([c_prompt, c_sample, jnp.zeros((r - n_c, d), F32)], axis=0)
    mod = _adaln(c_all, w_ada, b_ada)

    def mods_of(layer, lo, hi):
        m = mod[layer, lo:hi][:, None, :]
        return [m[..., i * d:(i + 1) * d] for i in range(6)]

    page, w2 = cache_moba_kv.shape[2], 2 * nh * hd
    cache_a = cache_moba_kv.reshape(depth, -1, page, w2)
    cache_d = cache_diff_kv.reshape(depth, -1, page, w2)

    yp, ys = x_prompt, x_sample
    outs = [[] for _ in range(8)]
    for layer in range(depth):
        lam_init = 0.8 - 0.6 * math.exp(-0.3 * layer)
        mp = mods_of(layer, 0, bp)
        ms = mods_of(layer, bp, n_c)
        x1p, kva_p, kvd_p, hl_p, cl_p = _mixers(yp, mp[0:3], layer, p, lam_init, None)
        past = (cache_a, cache_d, page_table, state_rglru_h[layer], state_rglru_conv[layer])
        x1s, kva_s, kvd_s, hl_s, cl_s = _mixers(ys, ms[0:3], layer, p, lam_init, past)
        yp, ys = _moe(x1p, x1s, mp[3:6], ms[3:6], layer, p)
        for lst, val in zip(outs, (kva_p.reshape(bp, sp, 2, nh, hd), kva_s.reshape(bs, ss, 2, nh, hd),
                                   kvd_p.reshape(bp, sp, 2, nh, hd), kvd_s.reshape(bs, ss, 2, nh, hd),
                                   hl_p, hl_s, cl_p, cl_s)):
            lst.append(val)
    return (yp, ys) + tuple(jnp.stack(o) for o in outs)
```

```python
import functools
import math

import jax
import jax.numpy as jnp
from jax import lax
from jax.experimental import pallas as pl
from jax.experimental.pallas import tpu as pltpu

F32 = jnp.float32
BF16 = jnp.bfloat16
I32 = jnp.int32
U32 = jnp.uint32
EPS = 1e-6
NEG_INF = float("-inf")
LOG2E = 1.4426950408889634

LANES = 128
HEAD_DIM = 128
MOBA_BLOCK = 256
MOBA_TOPK = 3
RGLRU_C = 8.0
TOP_K = 8
N_GROUPS = 8
TOPK_GROUPS = 4
ROUTED_SCALE = 2.5
VMEM_LIMIT_BYTES = 56 * 1024 * 1024
QUERY_LANES = LANES
MOBA_UNROLL = 4
DIFF_UNROLL = 2

_NT = (((1,), (1,)), ((), ()))
_NN = (((1,), (0,)), ((), ()))


def _params(n_axes):
    return pltpu.CompilerParams(dimension_semantics=("arbitrary",) * n_axes,
                                vmem_limit_bytes=VMEM_LIMIT_BYTES)


def _tile(n, pref):
    return pref if n % pref == 0 else n


def _split_bf16(x):
    hi = x.astype(BF16)
    lo = (x - hi.astype(F32)).astype(BF16)
    return hi, lo


def _dot3(a, b, dims):
    ah, al = _split_bf16(a)
    bh, bl = _split_bf16(b)
    dg = functools.partial(lax.dot_general, dimension_numbers=dims, preferred_element_type=F32)
    return dg(ah, bh) + dg(ah, bl) + dg(al, bh)


def _pack_pair(lo, hi):
    lo_bits = lax.bitcast_convert_type(lo.astype(BF16).astype(F32), U32) >> 16
    hi_bits = lax.bitcast_convert_type(hi.astype(BF16).astype(F32), U32) & jnp.uint32(0xFFFF0000)
    return hi_bits | lo_bits


def _unpack_pair(u):
    lo = lax.bitcast_convert_type(u << 16, F32)
    hi = lax.bitcast_convert_type(u & jnp.uint32(0xFFFF0000), F32)
    return lo, hi


def _group_ms(yc, gs):
    sq = yc * yc
    if gs == LANES:
        return jnp.mean(sq, axis=-1, keepdims=True)
    lane = lax.broadcasted_iota(I32, sq.shape, 1)
    lo = lane < gs
    s_lo = jnp.sum(jnp.where(lo, sq, 0.0), axis=-1, keepdims=True)
    s_hi = jnp.sum(jnp.where(lo, 0.0, sq), axis=-1, keepdims=True)
    return jnp.where(lo, s_lo, s_hi) * (1.0 / gs)


def _adaln_kernel(c_ref, w_ref, b_ref, o_ref):
    c = c_ref[...]
    s = (c * jax.nn.sigmoid(c)).astype(BF16)
    o_ref[...] = jnp.dot(s, w_ref[...].astype(BF16), preferred_element_type=F32) + b_ref[...]


def _adaln(c_all, w_ada, b_ada):
    depth, d, n = w_ada.shape
    r = c_all.shape[0]
    tn = _tile(n, 1024)
    return pl.pallas_call(
        _adaln_kernel,
        grid=(depth, n // tn),
        in_specs=[pl.BlockSpec((r, d), lambda l, j: (0, 0)),
                  pl.BlockSpec((None, d, tn), lambda l, j: (l, 0, j)),
                  pl.BlockSpec((None, 1, tn), lambda l, j: (l, 0, j))],
        out_specs=pl.BlockSpec((None, r, tn), lambda l, j: (l, 0, j)),
        out_shape=jax.ShapeDtypeStruct((depth, r, n), F32),
        compiler_params=_params(2),
        name="adaln",
    )(c_all, w_ada, b_ada.reshape(depth, 1, n))


def _norm_mod(x, g, sc, sh):
    y = x * lax.rsqrt(jnp.mean(x * x, axis=-1, keepdims=True) + EPS) * g
    return y * (1.0 + sc) + sh


def _norm1_kernel(x_ref, g_ref, sc_ref, sh_ref, o_ref):
    o_ref[...] = _norm_mod(x_ref[...], g_ref[...], sc_ref[...], sh_ref[...]).astype(o_ref.dtype)


def _norm1(x, g, sc, sh):
    b, s, d = x.shape
    tm = _tile(s, 512)
    mod_spec = pl.BlockSpec((None, 1, d), lambda bi, i: (bi, 0, 0))
    return pl.pallas_call(
        _norm1_kernel,
        grid=(b, s // tm),
        in_specs=[pl.BlockSpec((None, tm, d), lambda bi, i: (bi, i, 0)),
                  pl.BlockSpec((1, d), lambda bi, i: (0, 0)), mod_spec, mod_spec],
        out_specs=pl.BlockSpec((None, tm, d), lambda bi, i: (bi, i, 0)),
        out_shape=jax.ShapeDtypeStruct((b, s, d), BF16),
        compiler_params=_params(2),
        name="norm1",
    )(x, g.reshape(1, d), sc, sh)


def _norm2_router_kernel(x_ref, g_ref, sc_ref, sh_ref, rw_ref, o_ref, lg_ref):
    h = _norm_mod(x_ref[...], g_ref[...], sc_ref[...], sh_ref[...])
    half = h.shape[1] // 2
    o_ref[...] = _pack_pair(h[:, :half], h[:, half:])
    lg_ref[...] = _dot3(h, rw_ref[...], _NN)


def _norm2_router(x, g, sc, sh, router_w):
    b, s, d = x.shape
    e = router_w.shape[-1]
    tm = _tile(s, 256)
    mod_spec = pl.BlockSpec((None, 1, d), lambda bi, i: (bi, 0, 0))
    return pl.pallas_call(
        _norm2_router_kernel,
        grid=(b, s // tm),
        in_specs=[pl.BlockSpec((None, tm, d), lambda bi, i: (bi, i, 0)),
                  pl.BlockSpec((1, d), lambda bi, i: (0, 0)), mod_spec, mod_spec,
                  pl.BlockSpec((d, e), lambda bi, i: (0, 0))],
        out_specs=[pl.BlockSpec((None, tm, d // 2), lambda bi, i: (bi, i, 0)),
                   pl.BlockSpec((None, tm, e), lambda bi, i: (bi, i, 0))],
        out_shape=[jax.ShapeDtypeStruct((b, s, d // 2), U32), jax.ShapeDtypeStruct((b, s, e), F32)],
        compiler_params=_params(2),
        name="norm2_router",
    )(x, g.reshape(1, d), sc, sh, router_w)


def _inproj_kernel(a_ref, w_ref, g_ref, o_ref, *rest, norm_ntiles, n_tiles, gs):
    ob_ref = rest[0] if len(rest) == 2 else None
    wb_ref = rest[-1]
    j = pl.program_id(0)

    @pl.when((pl.program_id(1) == 0) & (pl.program_id(2) == 0))
    def _():
        wb_ref[...] = w_ref[...].astype(BF16)

    y = jnp.dot(a_ref[...], wb_ref[...], preferred_element_type=F32)

    def write(sl, val):
        o_ref[:, sl] = val.astype(o_ref.dtype)
        if ob_ref is not None:
            ob_ref[:, sl] = val.astype(BF16)

    def write_normed():
        g = g_ref[...]
        for c in range(y.shape[1] // LANES):
            sl = slice(c * LANES, (c + 1) * LANES)
            yc = y[:, sl]
            write(sl, yc * lax.rsqrt(_group_ms(yc, gs) + EPS) * g[:, sl])

    def write_raw():
        write(slice(None), y)

    if norm_ntiles == 0:
        write_raw()
    elif norm_ntiles == n_tiles:
        write_normed()
    else:
        pl.when(j < norm_ntiles)(write_normed)
        pl.when(j >= norm_ntiles)(write_raw)


def _inproj(h, w_in, layer, col0, ncols, tn, out_dtype, gain=None, gs=LANES, norm_ntiles=0, bf16_copy=False):
    b, s, k = h.shape
    tm = _tile(s, 512)
    n_tiles = ncols // tn
    cb = col0 // tn
    if gain is None:
        gain = jnp.ones((1, tn), F32)
    out_spec = pl.BlockSpec((None, tm, tn), lambda j, bi, i: (bi, i, j))
    out_shape = jax.ShapeDtypeStruct((b, s, ncols), out_dtype)
    return pl.pallas_call(
        functools.partial(_inproj_kernel, norm_ntiles=norm_ntiles, n_tiles=n_tiles, gs=gs),
        grid=(n_tiles, b, s // tm),
        in_specs=[pl.BlockSpec((None, tm, k), lambda j, bi, i: (bi, i, 0)),
                  pl.BlockSpec((None, k, tn), lambda j, bi, i: (layer, 0, cb + j)),
                  pl.BlockSpec((1, tn), lambda j, bi, i: (0, 0))],
        out_specs=[out_spec, out_spec] if bf16_copy else out_spec,
        out_shape=[out_shape, jax.ShapeDtypeStruct((b, s, ncols), BF16)] if bf16_copy else out_shape,
        scratch_shapes=[pltpu.VMEM((k, tn), BF16)],
        compiler_params=_params(3),
        name="inproj",
    )(h, w_in, gain)


def _kmean_kernel(k_ref, o_ref, *, nb):
    x = k_ref[...]
    o_ref[...] = jnp.mean(x.reshape(nb, MOBA_BLOCK, x.shape[-1]), axis=1)


def _kmean(kv, w):
    b, s, _ = kv.shape
    nf = s // MOBA_BLOCK
    nb = _tile(nf, 8)
    return pl.pallas_call(
        functools.partial(_kmean_kernel, nb=nb),
        grid=(b, nf // nb),
        in_specs=[pl.BlockSpec((None, nb * MOBA_BLOCK, w), lambda bi, i: (bi, i, 0))],
        out_specs=pl.BlockSpec((None, nb, w), lambda bi, i: (bi, i, 0)),
        out_shape=jax.ShapeDtypeStruct((b, nf, w), F32),
        compiler_params=_params(2),
        name="moba_kmean",
    )(kv)


def _select_topk_lowest_index(gate, idx, n_idx, k, axis):
    sel = jnp.zeros(gate.shape, jnp.bool_)
    for _ in range(k):
        m = jnp.max(gate, axis=axis, keepdims=True)
        first = jnp.min(jnp.where(gate == m, idx, n_idx), axis=axis, keepdims=True)
        pick = (idx == first) & (m > NEG_INF)
        sel = sel | pick
        gate = jnp.where(pick, NEG_INF, gate)
    return sel


def _fill_vt(v_ref, vt_ref, tk):
    def body(c, _):
        r0 = pl.multiple_of(c * tk, tk)
        vt_ref[:, pl.ds(r0, tk)] = v_ref[pl.ds(r0, tk), :].astype(F32).T.astype(BF16)
        return 0

    lax.fori_loop(0, v_ref.shape[0] // tk, body, 0)


def _flash_streams(streams, k_ref, vt_ref, n_iters, unroll, n_blocks, tk, mask_fn):
    d = streams[0].shape[0]
    ql = QUERY_LANES

    def body(j, carry):
        carry = list(carry)
        for u in range(unroll):
            n = j * unroll + u
            r0 = pl.multiple_of(jnp.minimum(n, n_blocks - 1) * tk, tk)
            kb = k_ref[pl.ds(r0, tk), :]
            vtb = vt_ref[:, pl.ds(r0, tk)]
            for si, q_t in enumerate(streams):
                m_i, l_i, acc = carry[si]
                s_t = jnp.dot(kb, q_t, preferred_element_type=F32)
                s_t = jnp.where(mask_fn(n, si), s_t, NEG_INF)
                m_new = jnp.maximum(m_i, jnp.max(s_t, axis=0, keepdims=True))
                m_safe = jnp.where(m_new == NEG_INF, 0.0, m_new)
                p = jnp.exp2(s_t - m_safe)
                alpha = jnp.exp2(m_i - m_safe)
                l_new = alpha * l_i + jnp.sum(p, axis=0, keepdims=True)
                acc = alpha * acc + jnp.dot(vtb, p.astype(BF16), preferred_element_type=F32)
                carry[si] = (m_new, l_new, acc)
        return tuple(carry)

    init = tuple((jnp.full((1, ql), NEG_INF, F32), jnp.zeros((1, ql), F32), jnp.zeros((d, ql), F32))
                 for _ in streams)
    out = lax.fori_loop(0, n_iters, body, init)
    return [(acc, l_i) for (_, l_i, acc) in out]


def _moba_prompt_kernel(q_ref, k_ref, v_ref, km_ref, o_ref, vt_ref, sel_ref, *, scale):
    i = pl.program_id(2)
    tk = MOBA_BLOCK
    ql = QUERY_LANES

    @pl.when(i == 0)
    def _():
        _fill_vt(v_ref, vt_ref, tk)

    q = q_ref[...]
    tq = q.shape[0]
    nf = km_ref.shape[0]
    q_t = q.T
    gate = _dot3(km_ref[...], q_t, _NN)
    n_iota = lax.broadcasted_iota(I32, gate.shape, 0)
    gate = jnp.where(n_iota < i, gate, NEG_INF)
    sel = _select_topk_lowest_index(gate, n_iota, nf, MOBA_TOPK, 0) | (n_iota == i)
    sel = sel.astype(F32)
    for r in range(nf):
        sel_ref[r] = sel[r:r + 1, :]
    for r in range(nf, sel_ref.shape[0]):
        sel_ref[r] = jnp.zeros((1, tq), F32)
    qs_t = (q_t * (scale * LOG2E)).astype(BF16)
    kr = lax.broadcasted_iota(I32, (tk, ql), 0)
    qc = lax.broadcasted_iota(I32, (tk, ql), 1)
    streams = [qs_t[:, h * ql:(h + 1) * ql] for h in range(tq // ql)]

    def mask_fn(n, si):
        causal = kr + (n - i) * tk <= qc + si * ql
        return causal & (sel_ref[n][:, si * ql:(si + 1) * ql] > 0.0)

    res = _flash_streams(streams, k_ref, vt_ref, (i + MOBA_UNROLL) // MOBA_UNROLL, MOBA_UNROLL, nf, tk, mask_fn)
    for h, (acc, l_i) in enumerate(res):
        o_ref[h * ql:(h + 1) * ql, :] = (acc / l_i).T.astype(o_ref.dtype)


def _moba_prompt(q, kvb, kmean):
    b, s, w = q.shape
    nh = w // HEAD_DIM
    nf = kmean.shape[1]
    return pl.pallas_call(
        functools.partial(_moba_prompt_kernel, scale=HEAD_DIM ** -0.5),
        grid=(b, nh, s // MOBA_BLOCK),
        in_specs=[pl.BlockSpec((None, MOBA_BLOCK, HEAD_DIM), lambda bi, h, i: (bi, i, h)),
                  pl.BlockSpec((None, s, HEAD_DIM), lambda bi, h, i: (bi, 0, h)),
                  pl.BlockSpec((None, s, HEAD_DIM), lambda bi, h, i: (bi, 0, nh + h)),
                  pl.BlockSpec((None, nf, HEAD_DIM), lambda bi, h, i: (bi, 0, h))],
        out_specs=pl.BlockSpec((None, MOBA_BLOCK, HEAD_DIM), lambda bi, h, i: (bi, i, h)),
        out_shape=jax.ShapeDtypeStruct((b, s, w), BF16),
        scratch_shapes=[pltpu.VMEM((HEAD_DIM, s), BF16), pltpu.VMEM((nf + MOBA_UNROLL, 1, MOBA_BLOCK), F32)],
        compiler_params=_params(3),
        name="moba_prompt",
    )(q, kvb, kvb, kmean)


def _diff_lambda(lq, lam_init):
    a = jnp.sum(lq[0:1] * lq[1:2], axis=-1, keepdims=True)
    b = jnp.sum(lq[2:3] * lq[3:4], axis=-1, keepdims=True)
    return jnp.exp(a) - jnp.exp(b) + lam_init


def _diff_prompt_kernel(q_ref, k_ref, v_ref, lq_ref, g_ref, o_ref, vt_ref, *, dd, lam_init, tk):
    i = pl.program_id(2)
    ql = QUERY_LANES

    @pl.when(i == 0)
    def _():
        _fill_vt(v_ref, vt_ref, tk)

    q = q_ref[...]
    tq = q.shape[0]
    nblk = k_ref.shape[0] // tk
    q_t = q.T * ((dd ** -0.5) * LOG2E)
    dr = lax.broadcasted_iota(I32, q_t.shape, 0)
    maps = [jnp.where(dr < dd, q_t, 0.0).astype(BF16), jnp.where(dr < dd, 0.0, q_t).astype(BF16)]
    lam = _diff_lambda(lq_ref[...], lam_init)
    kr = lax.broadcasted_iota(I32, (tk, ql), 0)
    qc = lax.broadcasted_iota(I32, (tk, ql), 1)
    n_half = tq // ql
    streams = [m_t[:, h * ql:(h + 1) * ql] for h in range(n_half) for m_t in maps]

    def mask_fn(n, si):
        return kr + (n - i) * tk <= qc + (si // 2) * ql

    res = _flash_streams(streams, k_ref, vt_ref, (i + DIFF_UNROLL) // DIFF_UNROLL, DIFF_UNROLL, nblk, tk, mask_fn)
    for h in range(n_half):
        (a1, l1), (a2, l2) = res[2 * h], res[2 * h + 1]
        dlt = (a1 / l1 - lam * (a2 / l2)).T
        ms = jnp.mean(dlt * dlt, axis=-1, keepdims=True)
        o_ref[h * ql:(h + 1) * ql, :] = (dlt * lax.rsqrt(ms + EPS) * g_ref[...] * (1.0 - lam_init)).astype(o_ref.dtype)


def _diff_prompt(q, kvb, diff_lambda, subln_g, lam_init):
    b, s, w = q.shape
    nh = w // HEAD_DIM
    dd = diff_lambda.shape[-1]
    tq = _tile(s, 256)
    return pl.pallas_call(
        functools.partial(_diff_prompt_kernel, dd=dd, lam_init=lam_init, tk=tq),
        grid=(b, nh, s // tq),
        in_specs=[pl.BlockSpec((None, tq, HEAD_DIM), lambda bi, h, i: (bi, i, h)),
                  pl.BlockSpec((None, s, HEAD_DIM), lambda bi, h, i: (bi, 0, h)),
                  pl.BlockSpec((None, s, HEAD_DIM), lambda bi, h, i: (bi, 0, nh + h)),
                  pl.BlockSpec(diff_lambda.shape, lambda bi, h, i: (0, 0)),
                  pl.BlockSpec((1, HEAD_DIM), lambda bi, h, i: (0, 0))],
        out_specs=pl.BlockSpec((None, tq, HEAD_DIM), lambda bi, h, i: (bi, i, h)),
        out_shape=jax.ShapeDtypeStruct((b, s, w), BF16),
        scratch_shapes=[pltpu.VMEM((HEAD_DIM, s), BF16)],
        compiler_params=_params(3),
        name="diff_prompt",
    )(q, kvb, kvb, diff_lambda, subln_g.reshape(1, HEAD_DIM))


def _page_specs(layer, pp, page, nh):
    return [pl.BlockSpec((None, None, page, 2, nh, HEAD_DIM),
                         functools.partial(lambda bi, j, pt, r: (layer, pt[bi, pp * j + r], 0, 0, 0, 0), r=r))
            for r in range(pp)]


def _flat_kv(pg_ref):
    page, _, nh, d = pg_ref.shape
    k = pg_ref[:, 0].reshape(page * nh, d).astype(BF16)
    v = pg_ref[:, 1].reshape(page * nh, d).astype(BF16)
    return k, v


def _head_match(rows, cols, rows_per_head, nh):
    r = lax.broadcasted_iota(I32, (rows, cols), 0)
    c = lax.broadcasted_iota(I32, (rows, cols), 1)
    return (r // rows_per_head) == (c % nh)


def _new_token_mask(rows, sq, rows_per_head, nh):
    r = lax.broadcasted_iota(I32, (rows, sq * nh), 0)
    c = lax.broadcasted_iota(I32, (rows, sq * nh), 1)
    return _head_match(rows, sq * nh, rows_per_head, nh) & ((c // nh) <= (r % sq))


def _paged_diff_kernel(pt_ref, q_ref, kn_ref, vn_ref, lq_ref, g_ref, *rest, nh, pp, sq, lam_init):
    pages = rest[:pp]
    o_ref, m_ref, l_ref, acc_ref = rest[pp:]
    j = pl.program_id(1)
    rows = q_ref.shape[0]
    page = pages[0].shape[0]
    w = page * nh

    @pl.when(j == 0)
    def _():
        m_ref[...] = jnp.full(m_ref.shape, NEG_INF, F32)
        l_ref[...] = jnp.zeros(l_ref.shape, F32)
        acc_ref[...] = jnp.zeros(acc_ref.shape, F32)

    q = q_ref[...]
    kv = [_flat_kv(pg) for pg in pages]
    s = jnp.concatenate([lax.dot_general(q, k, _NT, preferred_element_type=F32) for k, _ in kv], axis=-1)
    valid = _head_match(rows, w, 2 * sq, nh)
    s = jnp.where(jnp.concatenate([valid] * pp, axis=-1), s, NEG_INF)
    m_old = m_ref[...]
    m_new = jnp.maximum(m_old, jnp.max(s, axis=-1, keepdims=True))
    p = jnp.exp2(s - m_new)
    alpha = jnp.exp2(m_old - m_new)
    l_ref[...] = alpha * l_ref[...] + jnp.sum(p, axis=-1, keepdims=True)
    pb = p.astype(BF16)
    pv = sum(jnp.dot(pb[:, r * w:(r + 1) * w], kv[r][1], preferred_element_type=F32) for r in range(pp))
    acc_ref[...] = alpha * acc_ref[...] + pv
    m_ref[...] = m_new

    @pl.when(j == pl.num_programs(1) - 1)
    def _():
        s_n = lax.dot_general(q, kn_ref[...].astype(BF16), _NT, preferred_element_type=F32)
        s_n = jnp.where(_new_token_mask(rows, sq, 2 * sq, nh), s_n, NEG_INF)
        m_o = m_ref[...]
        m_f = jnp.maximum(m_o, jnp.max(s_n, axis=-1, keepdims=True))
        p_n = jnp.exp2(s_n - m_f)
        a_f = jnp.exp2(m_o - m_f)
        l_f = a_f * l_ref[...] + jnp.sum(p_n, axis=-1, keepdims=True)
        acc = a_f * acc_ref[...] + jnp.dot(p_n.astype(BF16), vn_ref[...].astype(BF16), preferred_element_type=F32)
        o = acc / l_f
        lam = _diff_lambda(lq_ref[...], lam_init)
        for h in range(nh):
            b0 = h * 2 * sq
            d = o[b0:b0 + sq] - lam * o[b0 + sq:b0 + 2 * sq]
            ms = jnp.mean(d * d, axis=-1, keepdims=True)
            o_ref[h * sq:(h + 1) * sq, :] = (d * lax.rsqrt(ms + EPS) * g_ref[...] * (1.0 - lam_init)).astype(o_ref.dtype)


def _head_rows(x, nh):
    db, sq, _ = x.shape
    return x.reshape(db, sq, nh, HEAD_DIM).transpose(0, 2, 1, 3).reshape(db, nh * sq, HEAD_DIM)


def _from_head_rows(o, sq):
    db, rows, _ = o.shape
    nh = rows // sq
    return o.reshape(db, nh, sq, HEAD_DIM).transpose(0, 2, 1, 3).reshape(db, sq, nh * HEAD_DIM)


def _paged_diff(q, kv_new, cache, layer, page_table, diff_lambda, subln_g, lam_init):
    db, sq, w = q.shape
    nh = w // HEAD_DIM
    dd = diff_lambda.shape[-1]
    n_pages = page_table.shape[1]
    page = cache.shape[2]
    pp = 4 if n_pages % 4 == 0 else 1
    rows = nh * 2 * sq
    q4 = q.reshape(db, sq, nh, HEAD_DIM).transpose(0, 2, 1, 3) * ((dd ** -0.5) * LOG2E)
    lane = jnp.arange(HEAD_DIM) < dd
    q_rows = jnp.stack([jnp.where(lane, q4, 0.0), jnp.where(lane, 0.0, q4)], axis=2)
    q_rows = q_rows.reshape(db, rows, HEAD_DIM).astype(BF16)
    kn = kv_new[..., :w].reshape(db, sq * nh, HEAD_DIM)
    vn = kv_new[..., w:].reshape(db, sq * nh, HEAD_DIM)
    new_spec = pl.BlockSpec((None, sq * nh, HEAD_DIM), lambda bi, j, pt: (bi, 0, 0))
    grid_spec = pltpu.PrefetchScalarGridSpec(
        num_scalar_prefetch=1,
        grid=(db, n_pages // pp),
        in_specs=[pl.BlockSpec((None, rows, HEAD_DIM), lambda bi, j, pt: (bi, 0, 0)), new_spec, new_spec,
                  pl.BlockSpec(diff_lambda.shape, lambda bi, j, pt: (0, 0)),
                  pl.BlockSpec((1, HEAD_DIM), lambda bi, j, pt: (0, 0))] + _page_specs(layer, pp, page, nh),
        out_specs=pl.BlockSpec((None, nh * sq, HEAD_DIM), lambda bi, j, pt: (bi, 0, 0)),
        scratch_shapes=[pltpu.VMEM((rows, 1), F32), pltpu.VMEM((rows, 1), F32), pltpu.VMEM((rows, HEAD_DIM), F32)],
    )
    o = pl.pallas_call(
        functools.partial(_paged_diff_kernel, nh=nh, pp=pp, sq=sq, lam_init=lam_init),
        grid_spec=grid_spec,
        out_shape=jax.ShapeDtypeStruct((db, nh * sq, HEAD_DIM), BF16),
        compiler_params=_params(2),
        name="diff_paged",
    )(page_table, q_rows, kn, vn, diff_lambda, subln_g.reshape(1, HEAD_DIM), *([cache] * pp))
    return _from_head_rows(o, sq)


def _paged_moba_kernel(pt_ref, q_ref, qf_ref, kn_ref, vn_ref, *rest, nh, bps, sq):
    pages = rest[:2 * bps]
    o_ref, m_ref, l_ref, o_acc_ref, g_ref = rest[2 * bps:]
    j = pl.program_id(1)
    nb = m_ref.shape[0]
    rows = q_ref.shape[0]
    page = pages[0].shape[0]
    w = page * nh
    q = q_ref[...]
    qf = qf_ref[...]
    valid = _head_match(rows, w, sq, nh)
    valid = jnp.concatenate([valid, valid], axis=-1)

    for blk in range(bps):
        n = j * bps + blk
        p0, p1 = pages[2 * blk], pages[2 * blk + 1]
        kmean = (jnp.sum(p0[:, 0], axis=0) + jnp.sum(p1[:, 0], axis=0)) * (1.0 / (2 * page))
        kmean_rows = jnp.concatenate([jnp.broadcast_to(kmean[h:h + 1], (sq, HEAD_DIM)) for h in range(nh)], axis=0)
        g_ref[n] = jnp.sum(qf * kmean_rows, axis=-1, keepdims=True)
        k0, v0 = _flat_kv(p0)
        k1, v1 = _flat_kv(p1)
        s = jnp.concatenate([lax.dot_general(q, k0, _NT, preferred_element_type=F32),
                             lax.dot_general(q, k1, _NT, preferred_element_type=F32)], axis=-1)
        s = jnp.where(valid, s, NEG_INF)
        m = jnp.max(s, axis=-1, keepdims=True)
        p = jnp.exp2(s - m)
        m_ref[n] = m
        l_ref[n] = jnp.sum(p, axis=-1, keepdims=True)
        pb = p.astype(BF16)
        o_acc_ref[n] = (jnp.dot(pb[:, :w], v0, preferred_element_type=F32)
                        + jnp.dot(pb[:, w:], v1, preferred_element_type=F32))

    @pl.when(j == pl.num_programs(1) - 1)
    def _():
        n_iota = lax.broadcasted_iota(I32, (nb, rows, 1), 0)
        sel = _select_topk_lowest_index(g_ref[...], n_iota, nb, MOBA_TOPK, 0)
        m_blk = jnp.where(sel, m_ref[...], NEG_INF)
        s_n = lax.dot_general(q, kn_ref[...].astype(BF16), _NT, preferred_element_type=F32)
        s_n = jnp.where(_new_token_mask(rows, sq, sq, nh), s_n, NEG_INF)
        m_all = jnp.maximum(jnp.max(m_blk, axis=0), jnp.max(s_n, axis=-1, keepdims=True))
        p_n = jnp.exp2(s_n - m_all)
        e_blk = jnp.where(sel, jnp.exp2(m_blk - m_all[None]), 0.0)
        l_all = jnp.sum(e_blk * l_ref[...], axis=0) + jnp.sum(p_n, axis=-1, keepdims=True)
        o_all = (jnp.sum(e_blk * o_acc_ref[...], axis=0)
                 + jnp.dot(p_n.astype(BF16), vn_ref[...].astype(BF16), preferred_element_type=F32))
        o_ref[...] = (o_all / l_all).astype(o_ref.dtype)


def _paged_moba(q, kv_new, cache, layer, page_table):
    db, sq, w = q.shape
    nh = w // HEAD_DIM
    n_pages = page_table.shape[1]
    page = cache.shape[2]
    assert 2 * page == MOBA_BLOCK and n_pages % 2 == 0 and sq <= MOBA_BLOCK
    nb = n_pages // 2
    bps = 2 if nb % 2 == 0 else 1
    pp = 2 * bps
    rows = nh * sq
    qf = _head_rows(q, nh)
    q_rows = (qf * ((HEAD_DIM ** -0.5) * LOG2E)).astype(BF16)
    kn = kv_new[..., :w].reshape(db, sq * nh, HEAD_DIM)
    vn = kv_new[..., w:].reshape(db, sq * nh, HEAD_DIM)
    row_spec = pl.BlockSpec((None, rows, HEAD_DIM), lambda bi, j, pt: (bi, 0, 0))
    new_spec = pl.BlockSpec((None, sq * nh, HEAD_DIM), lambda bi, j, pt: (bi, 0, 0))
    grid_spec = pltpu.PrefetchScalarGridSpec(
        num_scalar_prefetch=1,
        grid=(db, nb // bps),
        in_specs=[row_spec, row_spec, new_spec, new_spec] + _page_specs(layer, pp, page, nh),
        out_specs=row_spec,
        scratch_shapes=[pltpu.VMEM((nb, rows, 1), F32), pltpu.VMEM((nb, rows, 1), F32),
                        pltpu.VMEM((nb, rows, HEAD_DIM), F32), pltpu.VMEM((nb, rows, 1), F32)],
    )
    o = pl.pallas_call(
        functools.partial(_paged_moba_kernel, nh=nh, bps=bps, sq=sq),
        grid_spec=grid_spec,
        out_shape=jax.ShapeDtypeStruct((db, rows, HEAD_DIM), BF16),
        compiler_params=_params(2),
        name="moba_paged",
    )(page_table, q_rows, qf, kn, vn, *([cache] * pp))
    return _from_head_rows(o, sq)


def _shift_rows(x, d, fill):
    row = lax.broadcasted_iota(I32, x.shape, 0)
    return jnp.where(row >= d, pltpu.roll(x, d, 0), fill)


def _rglru_kernel(x_ref, h0_ref, c0_ref, cw_ref, cb_ref, wa_ref, ba_ref, wx_ref, bx_ref, lam_ref,
                  o_ref, hl_ref, cl_ref, hc_ref, tail_ref, *, cw_len):
    c = pl.program_id(2)
    nc = pl.num_programs(2)
    x = x_ref[...]
    tc = x.shape[0]
    nt = cw_len - 1

    @pl.when(c == 0)
    def _():
        hc_ref[...] = h0_ref[...]
        tail_ref[...] = c0_ref[...]

    tail = tail_ref[...]
    row = lax.broadcasted_iota(I32, x.shape, 0)
    cw = cw_ref[...]
    u = cb_ref[...] + x * cw[nt:nt + 1]
    for d in range(1, cw_len):
        fill = jnp.zeros_like(x)
        for r in range(d):
            fill = jnp.where(row == r, tail[nt - d + r:nt - d + r + 1], fill)
        u = u + _shift_rows(x, d, fill) * cw[nt - d:nt - d + 1]

    ub = u.astype(BF16)
    r_g = jax.nn.sigmoid(jnp.dot(ub, wa_ref[...].astype(BF16), preferred_element_type=F32) + ba_ref[...])
    i_g = jax.nn.sigmoid(jnp.dot(ub, wx_ref[...].astype(BF16), preferred_element_type=F32) + bx_ref[...])
    lam = lam_ref[...]
    log_sig = jnp.minimum(lam, 0.0) - jnp.log1p(jnp.exp(-jnp.abs(lam)))
    log_a = RGLRU_C * r_g * log_sig
    a = jnp.exp(log_a)
    bv = jnp.sqrt(-jnp.tanh(log_a) * (a * a + 1.0)) * (i_g * u)

    d = 1
    while d < tc:
        a_prev = _shift_rows(a, d, jnp.ones_like(a))
        b_prev = _shift_rows(bv, d, jnp.zeros_like(bv))
        bv = a * b_prev + bv
        a = a * a_prev
        d *= 2
    h = a * hc_ref[...] + bv
    o_ref[...] = h.astype(o_ref.dtype)
    hc_ref[...] = h[tc - 1:tc]
    if tc >= nt:
        tail_ref[...] = x[tc - nt:tc]
    else:
        tail_ref[...] = jnp.concatenate([tail[tc:], x], axis=0)

    @pl.when(c == nc - 1)
    def _():
        hl_ref[...] = hc_ref[...]
        cl_ref[...] = tail_ref[...]


def _rglru(xr, h0, conv0, conv_w, conv_b, w_a, b_a, w_x, b_x, lam):
    b, s, w = xr.shape
    nblk, bw, _ = w_a.shape
    assert bw == LANES
    cw_len = conv_w.shape[0]
    tc = _tile(s, 256)
    vec = lambda a: a.reshape(1, w)
    vspec = pl.BlockSpec((1, bw), lambda bi, n, c: (0, n))
    wspec = pl.BlockSpec((None, bw, bw), lambda bi, n, c: (n, 0, 0))
    return pl.pallas_call(
        functools.partial(_rglru_kernel, cw_len=cw_len),
        grid=(b, nblk, s // tc),
        in_specs=[pl.BlockSpec((None, tc, bw), lambda bi, n, c: (bi, c, n)),
                  pl.BlockSpec((None, 1, bw), lambda bi, n, c: (bi, 0, n)),
                  pl.BlockSpec((None, cw_len - 1, bw), lambda bi, n, c: (bi, 0, n)),
                  pl.BlockSpec((cw_len, bw), lambda bi, n, c: (0, n)),
                  vspec, wspec, vspec, wspec, vspec, vspec],
        out_specs=[pl.BlockSpec((None, tc, bw), lambda bi, n, c: (bi, c, n)),
                   pl.BlockSpec((None, 1, bw), lambda bi, n, c: (bi, 0, n)),
                   pl.BlockSpec((None, cw_len - 1, bw), lambda bi, n, c: (bi, 0, n))],
        out_shape=[jax.ShapeDtypeStruct((b, s, w), BF16),
                   jax.ShapeDtypeStruct((b, 1, w), F32),
                   jax.ShapeDtypeStruct((b, cw_len - 1, w), F32)],
        scratch_shapes=[pltpu.VMEM((1, bw), F32), pltpu.VMEM((cw_len - 1, bw), F32)],
        compiler_params=_params(3),
        name="rglru",
    )(xr, h0.reshape(b, 1, w), conv0, conv_w, vec(conv_b), w_a, vec(b_a), w_x, vec(b_x), vec(lam))


def _merge_kernel(a0_ref, a1_ref, a2_ref, g0_ref, g1_ref, g2_ref, w_ref, o_ref, wb_ref):
    @pl.when((pl.program_id(1) == 0) & (pl.program_id(2) == 0))
    def _():
        wb_ref[...] = w_ref[...].astype(BF16)

    acc = None
    for n, (a_ref, g_ref) in enumerate(((a0_ref, g0_ref), (a1_ref, g1_ref), (a2_ref, g2_ref))):
        y = jnp.dot(a_ref[...], wb_ref[n], preferred_element_type=F32) * jax.nn.sigmoid(g_ref[...])
        acc = y if acc is None else acc + y
    o_ref[...] = acc.astype(o_ref.dtype)


def _merge(oa, od, orr, gl, w_branch, layer):
    b, s, bw = oa.shape
    d = w_branch.shape[-1]
    tm = _tile(s, 512)
    tn = _tile(d, 512)
    nt = d // tn
    a_spec = pl.BlockSpec((None, tm, bw), lambda j, bi, i: (bi, i, 0))
    g_specs = [pl.BlockSpec((None, tm, tn), functools.partial(lambda j, bi, i, n: (bi, i, n * nt + j), n=n))
               for n in range(3)]
    return pl.pallas_call(
        _merge_kernel,
        grid=(nt, b, s // tm),
        in_specs=[a_spec, a_spec, a_spec] + g_specs
                 + [pl.BlockSpec((None, 3, bw, tn), lambda j, bi, i: (layer, 0, 0, j))],
        out_specs=pl.BlockSpec((None, tm, tn), lambda j, bi, i: (bi, i, j)),
        out_shape=jax.ShapeDtypeStruct((b, s, d), BF16),
        scratch_shapes=[pltpu.VMEM((3, bw, tn), BF16)],
        compiler_params=_params(3),
        name="branch_merge",
    )(oa, od, orr, gl, gl, gl, w_branch)


def _outproj_kernel(a_ref, w_ref, x_ref, g_ref, o_ref, wb_ref):
    @pl.when((pl.program_id(1) == 0) & (pl.program_id(2) == 0))
    def _():
        wb_ref[...] = w_ref[...].astype(BF16)

    y = jnp.dot(a_ref[...], wb_ref[...], preferred_element_type=F32)
    o_ref[...] = x_ref[...] + g_ref[...] * y


def _outproj(merged, w_o, layer, x, g1):
    b, s, d = x.shape
    tm = _tile(s, 512)
    tn = _tile(d, 1024)
    return pl.pallas_call(
        _outproj_kernel,
        grid=(d // tn, b, s // tm),
        in_specs=[pl.BlockSpec((None, tm, d), lambda j, bi, i: (bi, i, 0)),
                  pl.BlockSpec((None, d, tn), lambda j, bi, i: (layer, 0, j)),
                  pl.BlockSpec((None, tm, tn), lambda j, bi, i: (bi, i, j)),
                  pl.BlockSpec((None, 1, tn), lambda j, bi, i: (bi, 0, j))],
        out_specs=pl.BlockSpec((None, tm, tn), lambda j, bi, i: (bi, i, j)),
        out_shape=jax.ShapeDtypeStruct((b, s, d), F32),
        scratch_shapes=[pltpu.VMEM((d, tn), BF16)],
        compiler_params=_params(3),
        name="outproj",
    )(merged, w_o, x, g1)


def _stack_rows(rows):
    n = rows[0].shape[1]
    r_iota = lax.broadcasted_iota(I32, (len(rows), n), 0)
    out = jnp.zeros((len(rows), n), rows[0].dtype)
    for k, row in enumerate(rows):
        out = jnp.where(r_iota == k, row, out)
    return out


def _route_kernel(lg_ref, bias_ref, e_ref, w_ref, rank_ref, cnt_ref, carry_ref, *, n_exp):
    i = pl.program_id(0)

    @pl.when(i == 0)
    def _():
        carry_ref[...] = jnp.zeros(carry_ref.shape, F32)

    tile = lg_ref.shape[1]
    pg = n_exp // N_GROUPS
    scores = jax.nn.sigmoid(lg_ref[...])
    biased = scores + bias_ref[...]
    sc3 = scores.reshape(N_GROUPS, pg, tile)
    b3 = biased.reshape(N_GROUPS, pg, tile)
    j_iota = lax.broadcasted_iota(I32, b3.shape, 1)
    eid = lax.broadcasted_iota(I32, b3.shape, 0) * pg + j_iota

    m1 = jnp.max(b3, axis=1, keepdims=True)
    f1 = jnp.min(jnp.where(b3 == m1, j_iota, pg), axis=1, keepdims=True)
    m2 = jnp.max(jnp.where(j_iota == f1, NEG_INF, b3), axis=1, keepdims=True)
    grp = m1 + m2
    g_iota = lax.broadcasted_iota(I32, grp.shape, 0)
    g_sel = _select_topk_lowest_index(grp, g_iota, N_GROUPS, TOPK_GROUPS, 0)
    work = jnp.where(g_sel, b3, NEG_INF)

    e_rows, w_rows = [], []
    multi_hot = jnp.zeros(b3.shape, F32)
    for _ in range(TOP_K):
        m = jnp.max(jnp.max(work, axis=1, keepdims=True), axis=0, keepdims=True)
        first = jnp.min(jnp.min(jnp.where(work == m, eid, n_exp), axis=1, keepdims=True), axis=0, keepdims=True)
        pick = eid == first
        w_k = jnp.sum(jnp.sum(jnp.where(pick, sc3, 0.0), axis=1, keepdims=True), axis=0, keepdims=True)
        work = jnp.where(pick, NEG_INF, work)
        multi_hot = multi_hot + pick.astype(F32)
        e_rows.append(first.reshape(1, tile))
        w_rows.append(w_k.reshape(1, tile))
    e_idx = _stack_rows(e_rows)
    w = _stack_rows(w_rows)
    w = w / jnp.sum(w, axis=0, keepdims=True) * ROUTED_SCALE

    mh = multi_hot.reshape(n_exp, tile)
    r_i = lax.broadcasted_iota(I32, (tile, tile), 0)
    c_i = lax.broadcasted_iota(I32, (tile, tile), 1)
    before = (r_i < c_i).astype(BF16)
    base = carry_ref[:, 0:1] + jnp.dot(mh.astype(BF16), before, preferred_element_type=F32)
    e_sub = lax.broadcasted_iota(I32, (n_exp, tile), 0)
    ranks = [jnp.sum(jnp.where(e_sub == e_rows[k], base, 0.0), axis=0, keepdims=True) for k in range(TOP_K)]
    e_ref[...] = e_idx
    w_ref[...] = w
    rank_ref[...] = _stack_rows(ranks).astype(I32)
    carry_ref[...] = carry_ref[...] + jnp.sum(mh, axis=1, keepdims=True)
    cnt_ref[...] = carry_ref[...]


def _route(logits_t, bias):
    n_exp, t = logits_t.shape
    tile = 256
    out_spec = pl.BlockSpec((TOP_K, tile), lambda i: (0, i))
    return pl.pallas_call(
        functools.partial(_route_kernel, n_exp=n_exp),
        grid=(t // tile,),
        in_specs=[pl.BlockSpec((n_exp, tile), lambda i: (0, i)),
                  pl.BlockSpec((n_exp, 1), lambda i: (0, 0))],
        out_specs=[out_spec, out_spec, out_spec, pl.BlockSpec((n_exp, LANES), lambda i: (0, 0))],
        out_shape=[jax.ShapeDtypeStruct((TOP_K, t), I32), jax.ShapeDtypeStruct((TOP_K, t), F32),
                   jax.ShapeDtypeStruct((TOP_K, t), I32), jax.ShapeDtypeStruct((n_exp, LANES), F32)],
        scratch_shapes=[pltpu.VMEM((n_exp, LANES), F32)],
        compiler_params=_params(1),
        name="route",
    )(logits_t, bias.reshape(n_exp, 1))


def _slot_kernel(e_ref, rank_ref, ps_ref, o_ref):
    e_idx = e_ref[...]
    n_exp = ps_ref.shape[0]
    tile = e_idx.shape[1]
    e_sub = lax.broadcasted_iota(I32, (n_exp, tile), 0)
    ps = ps_ref[...]
    rows = [jnp.sum(jnp.where(e_sub == e_idx[k:k + 1], ps, 0), axis=0, keepdims=True) for k in range(TOP_K)]
    o_ref[...] = _stack_rows(rows) + rank_ref[...]


def _slots(e_idx, rank, p_start):
    _, t = e_idx.shape
    n_exp = p_start.shape[0]
    tile = 256
    spec = pl.BlockSpec((TOP_K, tile), lambda i: (0, i))
    return pl.pallas_call(
        _slot_kernel,
        grid=(t // tile,),
        in_specs=[spec, spec, pl.BlockSpec((n_exp, 1), lambda i: (0, 0))],
        out_specs=spec,
        out_shape=jax.ShapeDtypeStruct((TOP_K, t), I32),
        compiler_params=_params(1),
        name="slots",
    )(e_idx, rank, p_start.reshape(n_exp, 1))


def _zero_tails_kernel(blk_ref, o_ref):
    o_ref[...] = jnp.zeros(o_ref.shape, o_ref.dtype)


def _zero_tails(last_block, n_slots, dw, tm):
    n_exp = last_block.shape[0]
    grid_spec = pltpu.PrefetchScalarGridSpec(
        num_scalar_prefetch=1, grid=(n_exp,), in_specs=[],
        out_specs=pl.BlockSpec((tm, dw), lambda e, blk: (blk[e], 0)))
    return pl.pallas_call(
        _zero_tails_kernel, grid_spec=grid_spec,
        out_shape=jax.ShapeDtypeStruct((n_slots, dw), U32),
        compiler_params=_params(1),
        name="zero_tails",
    )(last_block)


def _row_copy(src_ref, src_row, dst_ref, dst_row, sem):
    return pltpu.make_async_copy(src_ref.at[pl.ds(src_row, 1), :], dst_ref.at[pl.ds(dst_row, 1), :], sem)


def _dispatch_kernel(pos_ref, h_ref, xs_in_ref, xs_ref, sem):
    del xs_in_ref
    tile = h_ref.shape[0]

    def issue(t, _):
        for k in range(TOP_K):
            _row_copy(h_ref, t, xs_ref, pos_ref[k, t], sem).start()
        return 0

    lax.fori_loop(0, tile, issue, 0)

    def drain(t, _):
        for k in range(TOP_K):
            _row_copy(h_ref, 0, xs_ref, 0, sem).wait()
        return 0

    lax.fori_loop(0, tile, drain, 0)


def _dispatch(pos_t, h_all, xs_init):
    t, dw = h_all.shape
    tile = 256
    return pl.pallas_call(
        _dispatch_kernel,
        grid=(t // tile,),
        in_specs=[pl.BlockSpec((TOP_K, tile), lambda i: (0, i), memory_space=pltpu.SMEM),
                  pl.BlockSpec((tile, dw), lambda i: (i, 0)),
                  pl.BlockSpec(memory_space=pl.ANY)],
        out_specs=pl.BlockSpec(memory_space=pl.ANY),
        out_shape=jax.ShapeDtypeStruct(xs_init.shape, xs_init.dtype),
        scratch_shapes=[pltpu.SemaphoreType.DMA],
        input_output_aliases={2: 0},
        compiler_params=_params(1),
        name="dispatch",
    )(pos_t, h_all, xs_init)


def _swiglu_packed(x_packed, wg_ref, wu_ref, wd_ref):
    lo, hi = _unpack_pair(x_packed)
    xl, xh = lo.astype(BF16), hi.astype(BF16)
    hw = xl.shape[1]
    g = (jnp.dot(xl, wg_ref[:hw], preferred_element_type=F32) + jnp.dot(xh, wg_ref[hw:], preferred_element_type=F32))
    u = (jnp.dot(xl, wu_ref[:hw], preferred_element_type=F32) + jnp.dot(xh, wu_ref[hw:], preferred_element_type=F32))
    mid = (g * jax.nn.sigmoid(g) * u).astype(BF16)
    return (jnp.dot(mid, wd_ref[:, :hw], preferred_element_type=F32),
            jnp.dot(mid, wd_ref[:, hw:], preferred_element_type=F32))


def _experts_kernel(be_ref, nu_ref, x_ref, wg_ref, wu_ref, wd_ref, o_ref, wgb_ref, wub_ref, wdb_ref):
    b = pl.program_id(0)
    prev = be_ref[jnp.maximum(b - 1, 0)]
    new_expert = (b == 0) | (be_ref[b] != prev)

    @pl.when(new_expert & (b < nu_ref[0]))
    def _():
        wgb_ref[...] = wg_ref[...].astype(BF16)
        wub_ref[...] = wu_ref[...].astype(BF16)
        wdb_ref[...] = wd_ref[...].astype(BF16)

    @pl.when(b < nu_ref[0])
    def _():
        y_lo, y_hi = _swiglu_packed(x_ref[...], wgb_ref, wub_ref, wdb_ref)
        o_ref[...] = _pack_pair(y_lo, y_hi)


def _experts(block_e, n_used, xs, w_gate, w_up, w_down, layer, tm):
    n_slots, dw = xs.shape
    d, de = w_gate.shape[-2:]
    n_blocks = n_slots // tm

    def row_map(b, be, nu):
        return (jnp.minimum(b, nu[0] - 1), 0)

    grid_spec = pltpu.PrefetchScalarGridSpec(
        num_scalar_prefetch=2,
        grid=(n_blocks,),
        in_specs=[pl.BlockSpec((tm, dw), row_map),
                  pl.BlockSpec((None, None, d, de), lambda b, be, nu: (layer, be[b], 0, 0)),
                  pl.BlockSpec((None, None, d, de), lambda b, be, nu: (layer, be[b], 0, 0)),
                  pl.BlockSpec((None, None, de, d), lambda b, be, nu: (layer, be[b], 0, 0))],
        out_specs=pl.BlockSpec((tm, dw), row_map),
        scratch_shapes=[pltpu.VMEM((d, de), BF16), pltpu.VMEM((d, de), BF16), pltpu.VMEM((de, d), BF16)],
    )
    return pl.pallas_call(
        _experts_kernel, grid_spec=grid_spec,
        out_shape=jax.ShapeDtypeStruct((n_slots, dw), U32),
        compiler_params=_params(1),
        name="experts",
    )(block_e, n_used, xs, w_gate, w_up, w_down)


def _combine_kernel(pos_ref, h_ref, x_ref, g_ref, w_ref, sg_ref, su_ref, sd_ref, ys_ref, o_ref,
                    buf_ref, sgb_ref, sub_ref, sdb_ref, sem):
    tile, hw = h_ref.shape

    @pl.when(pl.program_id(0) == 0)
    def _():
        sgb_ref[...] = sg_ref[...].astype(BF16)
        sub_ref[...] = su_ref[...].astype(BF16)
        sdb_ref[...] = sd_ref[...].astype(BF16)

    def issue(t, _):
        for k in range(TOP_K):
            _row_copy(ys_ref, pos_ref[k, t], buf_ref.at[k], t, sem).start()
        return 0

    lax.fori_loop(0, tile, issue, 0)

    acc_lo, acc_hi = _swiglu_packed(h_ref[...], sgb_ref, sub_ref, sdb_ref)

    def drain(t, _):
        for k in range(TOP_K):
            _row_copy(ys_ref, 0, buf_ref.at[k], 0, sem).wait()
        return 0

    lax.fori_loop(0, tile, drain, 0)

    w = w_ref[...]
    for k in range(TOP_K):
        lo, hi = _unpack_pair(buf_ref[k])
        acc_lo = acc_lo + w[:, k:k + 1] * lo
        acc_hi = acc_hi + w[:, k:k + 1] * hi
    g = g_ref[...]
    o_ref[:, :hw] = x_ref[:, :hw] + g[:, :hw] * acc_lo
    o_ref[:, hw:] = x_ref[:, hw:] + g[:, hw:] * acc_hi


def _combine(pos_t, h_all, row0, x, g2, w_tok, ys, s_gate, s_up, s_down, layer):
    t, d = x.shape
    dw = h_all.shape[1]
    de = s_gate.shape[-1]
    tile = 128
    b0 = row0 // tile
    g_rows = g2.shape[1]
    return pl.pallas_call(
        _combine_kernel,
        grid=(t // tile,),
        in_specs=[pl.BlockSpec((TOP_K, tile), lambda i: (0, b0 + i), memory_space=pltpu.SMEM),
                  pl.BlockSpec((tile, dw), lambda i: (b0 + i, 0)),
                  pl.BlockSpec((tile, d), lambda i: (i, 0)),
                  pl.BlockSpec((None, g_rows, d), lambda i: (i, 0, 0)),
                  pl.BlockSpec((tile, TOP_K), lambda i: (b0 + i, 0)),
                  pl.BlockSpec((None, d, de), lambda i: (layer, 0, 0)),
                  pl.BlockSpec((None, d, de), lambda i: (layer, 0, 0)),
                  pl.BlockSpec((None, de, d), lambda i: (layer, 0, 0)),
                  pl.BlockSpec(memory_space=pl.ANY)],
        out_specs=pl.BlockSpec((tile, d), lambda i: (i, 0)),
        out_shape=jax.ShapeDtypeStruct((t, d), F32),
        scratch_shapes=[pltpu.VMEM((TOP_K, tile, dw), U32), pltpu.VMEM((d, de), BF16),
                        pltpu.VMEM((d, de), BF16), pltpu.VMEM((de, d), BF16), pltpu.SemaphoreType.DMA],
        compiler_params=_params(1),
        name="combine",
    )(pos_t, h_all, x, g2, w_tok, s_gate, s_up, s_down, ys)


def _mixers(x, mods, layer, p, lam_init, past):
    b, s, d = x.shape
    w = p["moba_q_norm_g"].shape[-1] * (d // 256)
    sh1, sc1, g1 = mods
    h = _norm1(x, p["norm1_g"][layer], sc1, sh1)
    nh = w // HEAD_DIM
    tile_gain = lambda g: jnp.tile(g.reshape(1, -1), (1, nh))
    w_in = p["w_in"]
    tn = w
    prompt = past is None
    dd = p["diff_dim"]
    qa = _inproj(h, w_in, layer, 0, w, tn, F32, tile_gain(p["moba_q_norm_g"][layer]), LANES, 1)
    kv_a = _inproj(h, w_in, layer, w, 2 * w, tn, F32, tile_gain(p["moba_k_norm_g"][layer]), LANES, 1, prompt)
    qd = _inproj(h, w_in, layer, 3 * w, w, tn, F32, tile_gain(p["diff_q_norm_g"][layer]), dd, 1)
    kv_d = _inproj(h, w_in, layer, 4 * w, 2 * w, tn, F32, tile_gain(p["diff_k_norm_g"][layer]), dd, 1, prompt)
    xr = _inproj(h, w_in, layer, 6 * w, w, tn, F32)
    gl = _inproj(h, w_in, layer, 7 * w, 3 * d, tn, F32)

    if prompt:
        kv_a, kvb_a = kv_a
        kv_d, kvb_d = kv_d
        oa = _moba_prompt(qa, kvb_a, _kmean(kv_a, w))
        od = _diff_prompt(qd, kvb_d, p["diff_lambda"][layer], p["diff_subln_g"][layer], lam_init)
        h0 = jnp.zeros((b, w), F32)
        conv0 = jnp.zeros((b, p["rglru_conv_w"].shape[1] - 1, w), F32)
    else:
        cache_a, cache_d, page_table, h0, conv0 = past
        oa = _paged_moba(qa, kv_a, cache_a, layer, page_table)
        od = _paged_diff(qd, kv_d, cache_d, layer, page_table, p["diff_lambda"][layer],
                         p["diff_subln_g"][layer], lam_init)
    orr, h_last, conv_last = _rglru(xr, h0, conv0, p["rglru_conv_w"][layer], p["rglru_conv_b"][layer],
                                    p["rglru_wa"][layer], p["rglru_ba"][layer], p["rglru_wx"][layer],
                                    p["rglru_bx"][layer], p["rglru_lambda"][layer])
    merged = _merge(oa, od, orr, gl, p["w_branch"], layer)
    x1 = _outproj(merged, p["w_o"], layer, x, g1)
    return x1, kv_a, kv_d, h_last.reshape(b, w), conv_last


def _moe(x1p, x1s, mods_p, mods_s, layer, p):
    bp, sp, d = x1p.shape
    bs, ss, _ = x1s.shape
    n_exp = p["router_w"].shape[-1]
    tp, ts = bp * sp, bs * ss
    dw = d // 2
    h2p, lgp = _norm2_router(x1p, p["norm2_g"][layer], mods_p[1], mods_p[0], p["router_w"][layer])
    h2s, lgs = _norm2_router(x1s, p["norm2_g"][layer], mods_s[1], mods_s[0], p["router_w"][layer])
    t_all = tp + -(-ts // 256) * 256
    pad = t_all - tp - ts
    h_all = jnp.concatenate([h2p.reshape(tp, dw), h2s.reshape(ts, dw), jnp.zeros((pad, dw), U32)], axis=0)
    lg_all = jnp.concatenate([lgp.reshape(tp, n_exp), lgs.reshape(ts, n_exp), jnp.zeros((pad, n_exp), F32)], axis=0)
    e_idx, w_t, rank, cnt = _route(lg_all.T, p["router_bias"][layer])

    tm = 256
    n_blocks = -(-(t_all * TOP_K) // tm) + n_exp
    counts = cnt[:, 0].astype(I32)
    padded = (counts + tm - 1) // tm * tm
    p_end = jnp.cumsum(padded)
    p_start = p_end - padded
    n_used = (p_end[-1:] // tm).astype(I32)
    block_row0 = jnp.arange(n_blocks, dtype=I32) * tm
    block_e = jnp.minimum(jnp.sum((p_end[None, :] <= block_row0[:, None]).astype(I32), axis=1), n_exp - 1)
    last_block = jnp.maximum(p_end // tm - 1, 0).astype(I32)
    pos_t = _slots(e_idx, rank, p_start.astype(I32))

    xs = _dispatch(pos_t, h_all, _zero_tails(last_block, n_blocks * tm, dw, tm))
    ys = _experts(block_e, n_used, xs, p["expert_w_gate"], p["expert_w_up"], p["expert_w_down"], layer, tm)

    w_tok = w_t.T
    tile = 128
    shared = (p["shared_w_gate"], p["shared_w_up"], p["shared_w_down"], layer)
    g2p = jnp.broadcast_to(mods_p[2][:, None], (bp, sp // tile, 1, d)).reshape(tp // tile, 1, d)
    yp = _combine(pos_t, h_all, 0, x1p.reshape(tp, d), g2p, w_tok, ys, *shared)
    ts_pad = -(-ts // tile) * tile
    x1s_pad = jnp.concatenate([x1s.reshape(ts, d), jnp.zeros((ts_pad - ts, d), F32)], axis=0)
    g2s = jnp.broadcast_to(mods_s[2], (bs, ss, d)).reshape(ts, d)
    g2s = jnp.concatenate([g2s, jnp.zeros((ts_pad - ts, d), F32)], axis=0).reshape(ts_pad // tile, tile, d)
    ysamp = _combine(pos_t, h_all, tp, x1s_pad, g2s, w_tok, ys, *shared)
    return yp.reshape(bp, sp, d), ysamp[:ts].reshape(bs, ss, d)


def kernel(x_prompt, x_sample, cache_moba_kv, cache_diff_kv, state_rglru_h, state_rglru_conv, page_table, c_prompt, c_sample, w_ada, b_ada, norm1_g, norm2_g, w_in, moba_q_norm_g, moba_k_norm_g, diff_q_norm_g, diff_k_norm_g, diff_lambda, diff_subln_g, rglru_conv_w, rglru_conv_b, rglru_wa, rglru_ba, rglru_wx, rglru_bx, rglru_lambda, w_branch, w_o, router_w, router_bias, expert_w_gate, expert_w_up, expert_w_down, shared_w_gate, shared_w_up, shared_w_down):
    depth = w_ada.shape[0]
    bp, sp, d = x_prompt.shape
    bs, ss, _ = x_sample.shape
    nh, hd = cache_moba_kv.shape[-2:]
    assert hd == HEAD_DIM and sp % MOBA_BLOCK == 0 and (bp * sp) % 256 == 0
    p = dict(norm1_g=norm1_g, norm2_g=norm2_g, w_in=w_in, moba_q_norm_g=moba_q_norm_g,
             moba_k_norm_g=moba_k_norm_g, diff_dim=diff_q_norm_g.shape[-1],
             diff_q_norm_g=diff_q_norm_g.reshape(depth, -1), diff_k_norm_g=diff_k_norm_g.reshape(depth, -1),
             diff_lambda=diff_lambda, diff_subln_g=diff_subln_g, rglru_conv_w=rglru_conv_w,
             rglru_conv_b=rglru_conv_b, rglru_wa=rglru_wa, rglru_ba=rglru_ba, rglru_wx=rglru_wx,
             rglru_bx=rglru_bx, rglru_lambda=rglru_lambda, w_branch=w_branch, w_o=w_o,
             router_w=router_w, router_bias=router_bias, expert_w_gate=expert_w_gate,
             expert_w_up=expert_w_up, expert_w_down=expert_w_down, shared_w_gate=shared_w_gate,
             shared_w_up=shared_w_up, shared_w_down=shared_w_down)
    n_c = bp + bs
    r = -(-n_c // 8) * 8
    c_all = jnp.concatenate([c_prompt, c_sample, jnp.zeros((r - n_c, d), F32)], axis=0)
    mod = _adaln(c_all, w_ada, b_ada)

    def mods_of(layer, lo, hi):
        m = mod[layer, lo:hi][:, None, :]
        return [m[..., i * d:(i + 1) * d] for i in range(6)]

    yp, ys = x_prompt, x_sample
    outs = [[] for _ in range(8)]
    for layer in range(depth):
        lam_init = 0.8 - 0.6 * math.exp(-0.3 * layer)
        mp = mods_of(layer, 0, bp)
        ms = mods_of(layer, bp, n_c)
        x1p, kva_p, kvd_p, hl_p, cl_p = _mixers(yp, mp[0:3], layer, p, lam_init, None)
        past = (cache_moba_kv, cache_diff_kv, page_table, state_rglru_h[layer], state_rglru_conv[layer])
        x1s, kva_s, kvd_s, hl_s, cl_s = _mixers(ys, ms[0:3], layer, p, lam_init, past)
        yp, ys = _moe(x1p, x1s, mp[3:6], ms[3:6], layer, p)
        for lst, val in zip(outs, (kva_p.reshape(bp, sp, 2, nh, hd), kva_s.reshape(bs, ss, 2, nh, hd),
                                   kvd_p.reshape(bp, sp, 2, nh, hd), kvd_s.reshape(bs, ss, 2, nh, hd),
                                   hl_p, hl_s, cl_p, cl_s)):
            lst.append(val)
    return (yp, ys) + tuple(jnp.stack(o) for o in outs)
```

```python
import functools
import math

import jax
import jax.numpy as jnp
from jax import lax
from jax.experimental import pallas as pl
from jax.experimental.pallas import tpu as pltpu

F32 = jnp.float32
BF16 = jnp.bfloat16
I32 = jnp.int32
U32 = jnp.uint32
EPS = 1e-6
NEG_INF = float("-inf")
LOG2E = 1.4426950408889634

LANES = 128
HEAD_DIM = 128
MOBA_BLOCK = 256
MOBA_TOPK = 3
RGLRU_C = 8.0
TOP_K = 8
N_GROUPS = 8
TOPK_GROUPS = 4
ROUTED_SCALE = 2.5
VMEM_LIMIT_BYTES = 56 * 1024 * 1024
QUERY_LANES = LANES
MOBA_UNROLL = 4
DIFF_UNROLL = 2
ROWS_PLAIN = 1024
ROWS_EPILOGUE = 512

_NT = (((1,), (1,)), ((), ()))
_NN = (((1,), (0,)), ((), ()))


def _params(n_axes):
    return pltpu.CompilerParams(dimension_semantics=("arbitrary",) * n_axes,
                                vmem_limit_bytes=VMEM_LIMIT_BYTES)


def _tile(n, pref):
    return pref if n % pref == 0 else n


def _split_bf16(x):
    hi = x.astype(BF16)
    lo = (x - hi.astype(F32)).astype(BF16)
    return hi, lo


def _dot3(a, b, dims):
    ah, al = _split_bf16(a)
    bh, bl = _split_bf16(b)
    dg = functools.partial(lax.dot_general, dimension_numbers=dims, preferred_element_type=F32)
    return dg(ah, bh) + dg(ah, bl) + dg(al, bh)


def _pack_pair(lo, hi):
    lo_bits = lax.bitcast_convert_type(lo.astype(BF16).astype(F32), U32) >> 16
    hi_bits = lax.bitcast_convert_type(hi.astype(BF16).astype(F32), U32) & jnp.uint32(0xFFFF0000)
    return hi_bits | lo_bits


def _unpack_pair(u):
    lo = lax.bitcast_convert_type(u << 16, F32)
    hi = lax.bitcast_convert_type(u & jnp.uint32(0xFFFF0000), F32)
    return lo, hi


def _group_ms(yc, gs):
    sq = yc * yc
    if gs == LANES:
        return jnp.mean(sq, axis=-1, keepdims=True)
    lane = lax.broadcasted_iota(I32, sq.shape, 1)
    lo = lane < gs
    s_lo = jnp.sum(jnp.where(lo, sq, 0.0), axis=-1, keepdims=True)
    s_hi = jnp.sum(jnp.where(lo, 0.0, sq), axis=-1, keepdims=True)
    return jnp.where(lo, s_lo, s_hi) * (1.0 / gs)


def _adaln_kernel(c_ref, w_ref, b_ref, o_ref):
    c = c_ref[...]
    s = (c * jax.nn.sigmoid(c)).astype(BF16)
    o_ref[...] = jnp.dot(s, w_ref[...].astype(BF16), preferred_element_type=F32) + b_ref[...]


def _adaln(c_all, w_ada, b_ada):
    depth, d, n = w_ada.shape
    r = c_all.shape[0]
    tn = _tile(n, 1024)
    return pl.pallas_call(
        _adaln_kernel,
        grid=(depth, n // tn),
        in_specs=[pl.BlockSpec((r, d), lambda l, j: (0, 0)),
                  pl.BlockSpec((None, d, tn), lambda l, j: (l, 0, j)),
                  pl.BlockSpec((None, 1, tn), lambda l, j: (l, 0, j))],
        out_specs=pl.BlockSpec((None, r, tn), lambda l, j: (l, 0, j)),
        out_shape=jax.ShapeDtypeStruct((depth, r, n), F32),
        compiler_params=_params(2),
        name="adaln",
    )(c_all, w_ada, b_ada.reshape(depth, 1, n))


def _norm_mod(x, g, sc, sh):
    y = x * lax.rsqrt(jnp.mean(x * x, axis=-1, keepdims=True) + EPS) * g
    return y * (1.0 + sc) + sh


def _norm1_kernel(x_ref, g_ref, sc_ref, sh_ref, o_ref):
    o_ref[...] = _norm_mod(x_ref[...], g_ref[...], sc_ref[...], sh_ref[...]).astype(o_ref.dtype)


def _norm1(x, g, sc, sh):
    b, s, d = x.shape
    tm = _tile(s, 512)
    mod_spec = pl.BlockSpec((None, 1, d), lambda bi, i: (bi, 0, 0))
    return pl.pallas_call(
        _norm1_kernel,
        grid=(b, s // tm),
        in_specs=[pl.BlockSpec((None, tm, d), lambda bi, i: (bi, i, 0)),
                  pl.BlockSpec((1, d), lambda bi, i: (0, 0)), mod_spec, mod_spec],
        out_specs=pl.BlockSpec((None, tm, d), lambda bi, i: (bi, i, 0)),
        out_shape=jax.ShapeDtypeStruct((b, s, d), BF16),
        compiler_params=_params(2),
        name="norm1",
    )(x, g.reshape(1, d), sc, sh)


def _norm2_router_kernel(x_ref, g_ref, sc_ref, sh_ref, rw_ref, o_ref, lg_ref):
    h = _norm_mod(x_ref[...], g_ref[...], sc_ref[...], sh_ref[...])
    half = h.shape[1] // 2
    o_ref[...] = _pack_pair(h[:, :half], h[:, half:])
    lg_ref[...] = _dot3(h, rw_ref[...], _NN)


def _norm2_router(x, g, sc, sh, router_w):
    b, s, d = x.shape
    e = router_w.shape[-1]
    tm = _tile(s, 256)
    mod_spec = pl.BlockSpec((None, 1, d), lambda bi, i: (bi, 0, 0))
    return pl.pallas_call(
        _norm2_router_kernel,
        grid=(b, s // tm),
        in_specs=[pl.BlockSpec((None, tm, d), lambda bi, i: (bi, i, 0)),
                  pl.BlockSpec((1, d), lambda bi, i: (0, 0)), mod_spec, mod_spec,
                  pl.BlockSpec((d, e), lambda bi, i: (0, 0))],
        out_specs=[pl.BlockSpec((None, tm, d // 2), lambda bi, i: (bi, i, 0)),
                   pl.BlockSpec((None, tm, e), lambda bi, i: (bi, i, 0))],
        out_shape=[jax.ShapeDtypeStruct((b, s, d // 2), U32), jax.ShapeDtypeStruct((b, s, e), F32)],
        compiler_params=_params(2),
        name="norm2_router",
    )(x, g.reshape(1, d), sc, sh, router_w)


def _inproj_kernel(a_ref, w_ref, g_ref, o_ref, *rest, norm_ntiles, n_tiles, gs):
    ob_ref = rest[0] if len(rest) == 2 else None
    wb_ref = rest[-1]
    j = pl.program_id(0)

    @pl.when((pl.program_id(1) == 0) & (pl.program_id(2) == 0))
    def _():
        wb_ref[...] = w_ref[...].astype(BF16)

    y = jnp.dot(a_ref[...], wb_ref[...], preferred_element_type=F32)

    def write(sl, val):
        o_ref[:, sl] = val.astype(o_ref.dtype)
        if ob_ref is not None:
            ob_ref[:, sl] = val.astype(BF16)

    def write_normed():
        g = g_ref[...]
        for c in range(y.shape[1] // LANES):
            sl = slice(c * LANES, (c + 1) * LANES)
            yc = y[:, sl]
            write(sl, yc * lax.rsqrt(_group_ms(yc, gs) + EPS) * g[:, sl])

    def write_raw():
        write(slice(None), y)

    if norm_ntiles == 0:
        write_raw()
    elif norm_ntiles == n_tiles:
        write_normed()
    else:
        pl.when(j < norm_ntiles)(write_normed)
        pl.when(j >= norm_ntiles)(write_raw)


def _inproj(h, w_in, layer, col0, ncols, tn, out_dtype, gain=None, gs=LANES, norm_ntiles=0, bf16_copy=False):
    b, s, k = h.shape
    tm = _tile(s, ROWS_PLAIN if norm_ntiles == 0 and not bf16_copy else ROWS_EPILOGUE)
    n_tiles = ncols // tn
    cb = col0 // tn
    if gain is None:
        gain = jnp.ones((1, tn), F32)
    out_spec = pl.BlockSpec((None, tm, tn), lambda j, bi, i: (bi, i, j))
    out_shape = jax.ShapeDtypeStruct((b, s, ncols), out_dtype)
    return pl.pallas_call(
        functools.partial(_inproj_kernel, norm_ntiles=norm_ntiles, n_tiles=n_tiles, gs=gs),
        grid=(n_tiles, b, s // tm),
        in_specs=[pl.BlockSpec((None, tm, k), lambda j, bi, i: (bi, i, 0)),
                  pl.BlockSpec((None, k, tn), lambda j, bi, i: (layer, 0, cb + j)),
                  pl.BlockSpec((1, tn), lambda j, bi, i: (0, 0))],
        out_specs=[out_spec, out_spec] if bf16_copy else out_spec,
        out_shape=[out_shape, jax.ShapeDtypeStruct((b, s, ncols), BF16)] if bf16_copy else out_shape,
        scratch_shapes=[pltpu.VMEM((k, tn), BF16)],
        compiler_params=_params(3),
        name="inproj",
    )(h, w_in, gain)


def _kmean_kernel(k_ref, o_ref, *, nb):
    x = k_ref[...]
    o_ref[...] = jnp.mean(x.reshape(nb, MOBA_BLOCK, x.shape[-1]), axis=1)


def _kmean(kv, w):
    b, s, _ = kv.shape
    nf = s // MOBA_BLOCK
    nb = _tile(nf, 8)
    return pl.pallas_call(
        functools.partial(_kmean_kernel, nb=nb),
        grid=(b, nf // nb),
        in_specs=[pl.BlockSpec((None, nb * MOBA_BLOCK, w), lambda bi, i: (bi, i, 0))],
        out_specs=pl.BlockSpec((None, nb, w), lambda bi, i: (bi, i, 0)),
        out_shape=jax.ShapeDtypeStruct((b, nf, w), F32),
        compiler_params=_params(2),
        name="moba_kmean",
    )(kv)


def _select_topk_lowest_index(gate, idx, n_idx, k, axis):
    sel = jnp.zeros(gate.shape, jnp.bool_)
    for _ in range(k):
        m = jnp.max(gate, axis=axis, keepdims=True)
        first = jnp.min(jnp.where(gate == m, idx, n_idx), axis=axis, keepdims=True)
        pick = (idx == first) & (m > NEG_INF)
        sel = sel | pick
        gate = jnp.where(pick, NEG_INF, gate)
    return sel


def _fill_vt(v_ref, vt_ref, tk):
    def body(c, _):
        r0 = pl.multiple_of(c * tk, tk)
        vt_ref[:, pl.ds(r0, tk)] = v_ref[pl.ds(r0, tk), :].astype(F32).T.astype(BF16)
        return 0

    lax.fori_loop(0, v_ref.shape[0] // tk, body, 0)


def _flash_streams(streams, k_ref, vt_ref, n_iters, unroll, n_blocks, tk, mask_fn):
    d = streams[0].shape[0]
    ql = QUERY_LANES

    def body(j, carry):
        carry = list(carry)
        for u in range(unroll):
            n = j * unroll + u
            r0 = pl.multiple_of(jnp.minimum(n, n_blocks - 1) * tk, tk)
            kb = k_ref[pl.ds(r0, tk), :]
            vtb = vt_ref[:, pl.ds(r0, tk)]
            for si, q_t in enumerate(streams):
                m_i, l_i, acc = carry[si]
                s_t = jnp.dot(kb, q_t, preferred_element_type=F32)
                s_t = jnp.where(mask_fn(n, si), s_t, NEG_INF)
                m_new = jnp.maximum(m_i, jnp.max(s_t, axis=0, keepdims=True))
                m_safe = jnp.where(m_new == NEG_INF, 0.0, m_new)
                p = jnp.exp2(s_t - m_safe)
                alpha = jnp.exp2(m_i - m_safe)
                l_new = alpha * l_i + jnp.sum(p, axis=0, keepdims=True)
                acc = alpha * acc + jnp.dot(vtb, p.astype(BF16), preferred_element_type=F32)
                carry[si] = (m_new, l_new, acc)
        return tuple(carry)

    init = tuple((jnp.full((1, ql), NEG_INF, F32), jnp.zeros((1, ql), F32), jnp.zeros((d, ql), F32))
                 for _ in streams)
    out = lax.fori_loop(0, n_iters, body, init)
    return [(acc, l_i) for (_, l_i, acc) in out]


def _moba_prompt_kernel(q_ref, k_ref, v_ref, km_ref, o_ref, vt_ref, sel_ref, *, scale):
    i = pl.program_id(2)
    tk = MOBA_BLOCK
    ql = QUERY_LANES

    @pl.when(i == 0)
    def _():
        _fill_vt(v_ref, vt_ref, tk)

    q = q_ref[...]
    tq = q.shape[0]
    nf = km_ref.shape[0]
    q_t = q.T
    gate = _dot3(km_ref[...], q_t, _NN)
    n_iota = lax.broadcasted_iota(I32, gate.shape, 0)
    gate = jnp.where(n_iota < i, gate, NEG_INF)
    sel = _select_topk_lowest_index(gate, n_iota, nf, MOBA_TOPK, 0) | (n_iota == i)
    sel = sel.astype(F32)
    for r in range(nf):
        sel_ref[r] = sel[r:r + 1, :]
    for r in range(nf, sel_ref.shape[0]):
        sel_ref[r] = jnp.zeros((1, tq), F32)
    qs_t = (q_t * (scale * LOG2E)).astype(BF16)
    kr = lax.broadcasted_iota(I32, (tk, ql), 0)
    qc = lax.broadcasted_iota(I32, (tk, ql), 1)
    streams = [qs_t[:, h * ql:(h + 1) * ql] for h in range(tq // ql)]

    def mask_fn(n, si):
        causal = kr + (n - i) * tk <= qc + si * ql
        return causal & (sel_ref[n][:, si * ql:(si + 1) * ql] > 0.0)

    res = _flash_streams(streams, k_ref, vt_ref, (i + MOBA_UNROLL) // MOBA_UNROLL, MOBA_UNROLL, nf, tk, mask_fn)
    for h, (acc, l_i) in enumerate(res):
        o_ref[h * ql:(h + 1) * ql, :] = (acc / l_i).T.astype(o_ref.dtype)


def _moba_prompt(q, kvb, kmean):
    b, s, w = q.shape
    nh = w // HEAD_DIM
    nf = kmean.shape[1]
    return pl.pallas_call(
        functools.partial(_moba_prompt_kernel, scale=HEAD_DIM ** -0.5),
        grid=(b, nh, s // MOBA_BLOCK),
        in_specs=[pl.BlockSpec((None, MOBA_BLOCK, HEAD_DIM), lambda bi, h, i: (bi, i, h)),
                  pl.BlockSpec((None, s, HEAD_DIM), lambda bi, h, i: (bi, 0, h)),
                  pl.BlockSpec((None, s, HEAD_DIM), lambda bi, h, i: (bi, 0, nh + h)),
                  pl.BlockSpec((None, nf, HEAD_DIM), lambda bi, h, i: (bi, 0, h))],
        out_specs=pl.BlockSpec((None, MOBA_BLOCK, HEAD_DIM), lambda bi, h, i: (bi, i, h)),
        out_shape=jax.ShapeDtypeStruct((b, s, w), BF16),
        scratch_shapes=[pltpu.VMEM((HEAD_DIM, s), BF16), pltpu.VMEM((nf + MOBA_UNROLL, 1, MOBA_BLOCK), F32)],
        compiler_params=_params(3),
        name="moba_prompt",
    )(q, kvb, kvb, kmean)


def _diff_lambda(lq, lam_init):
    a = jnp.sum(lq[0:1] * lq[1:2], axis=-1, keepdims=True)
    b = jnp.sum(lq[2:3] * lq[3:4], axis=-1, keepdims=True)
    return jnp.exp(a) - jnp.exp(b) + lam_init


def _diff_prompt_kernel(q_ref, k_ref, v_ref, lq_ref, g_ref, o_ref, vt_ref, *, dd, lam_init, tk):
    i = pl.program_id(2)
    ql = QUERY_LANES

    @pl.when(i == 0)
    def _():
        _fill_vt(v_ref, vt_ref, tk)

    q = q_ref[...]
    tq = q.shape[0]
    nblk = k_ref.shape[0] // tk
    q_t = q.T * ((dd ** -0.5) * LOG2E)
    dr = lax.broadcasted_iota(I32, q_t.shape, 0)
    maps = [jnp.where(dr < dd, q_t, 0.0).astype(BF16), jnp.where(dr < dd, 0.0, q_t).astype(BF16)]
    lam = _diff_lambda(lq_ref[...], lam_init)
    kr = lax.broadcasted_iota(I32, (tk, ql), 0)
    qc = lax.broadcasted_iota(I32, (tk, ql), 1)
    n_half = tq // ql
    streams = [m_t[:, h * ql:(h + 1) * ql] for h in range(n_half) for m_t in maps]

    def mask_fn(n, si):
        return kr + (n - i) * tk <= qc + (si // 2) * ql

    res = _flash_streams(streams, k_ref, vt_ref, (i + DIFF_UNROLL) // DIFF_UNROLL, DIFF_UNROLL, nblk, tk, mask_fn)
    for h in range(n_half):
        (a1, l1), (a2, l2) = res[2 * h], res[2 * h + 1]
        dlt = (a1 / l1 - lam * (a2 / l2)).T
        ms = jnp.mean(dlt * dlt, axis=-1, keepdims=True)
        o_ref[h * ql:(h + 1) * ql, :] = (dlt * lax.rsqrt(ms + EPS) * g_ref[...] * (1.0 - lam_init)).astype(o_ref.dtype)


def _diff_prompt(q, kvb, diff_lambda, subln_g, lam_init):
    b, s, w = q.shape
    nh = w // HEAD_DIM
    dd = diff_lambda.shape[-1]
    tq = _tile(s, 256)
    return pl.pallas_call(
        functools.partial(_diff_prompt_kernel, dd=dd, lam_init=lam_init, tk=tq),
        grid=(b, nh, s // tq),
        in_specs=[pl.BlockSpec((None, tq, HEAD_DIM), lambda bi, h, i: (bi, i, h)),
                  pl.BlockSpec((None, s, HEAD_DIM), lambda bi, h, i: (bi, 0, h)),
                  pl.BlockSpec((None, s, HEAD_DIM), lambda bi, h, i: (bi, 0, nh + h)),
                  pl.BlockSpec(diff_lambda.shape, lambda bi, h, i: (0, 0)),
                  pl.BlockSpec((1, HEAD_DIM), lambda bi, h, i: (0, 0))],
        out_specs=pl.BlockSpec((None, tq, HEAD_DIM), lambda bi, h, i: (bi, i, h)),
        out_shape=jax.ShapeDtypeStruct((b, s, w), BF16),
        scratch_shapes=[pltpu.VMEM((HEAD_DIM, s), BF16)],
        compiler_params=_params(3),
        name="diff_prompt",
    )(q, kvb, kvb, diff_lambda, subln_g.reshape(1, HEAD_DIM))


def _page_specs(layer, pp, page, nh):
    return [pl.BlockSpec((None, None, page, 2, nh, HEAD_DIM),
                         functools.partial(lambda bi, j, pt, r: (layer, pt[bi, pp * j + r], 0, 0, 0, 0), r=r))
            for r in range(pp)]


def _flat_kv(pg_ref):
    page, _, nh, d = pg_ref.shape
    k = pg_ref[:, 0].reshape(page * nh, d).astype(BF16)
    v = pg_ref[:, 1].reshape(page * nh, d).astype(BF16)
    return k, v


def _head_match(rows, cols, rows_per_head, nh):
    r = lax.broadcasted_iota(I32, (rows, cols), 0)
    c = lax.broadcasted_iota(I32, (rows, cols), 1)
    return (r // rows_per_head) == (c % nh)


def _new_token_mask(rows, sq, rows_per_head, nh):
    r = lax.broadcasted_iota(I32, (rows, sq * nh), 0)
    c = lax.broadcasted_iota(I32, (rows, sq * nh), 1)
    return _head_match(rows, sq * nh, rows_per_head, nh) & ((c // nh) <= (r % sq))


def _paged_diff_kernel(pt_ref, q_ref, kn_ref, vn_ref, lq_ref, g_ref, *rest, nh, pp, sq, lam_init):
    pages = rest[:pp]
    o_ref, m_ref, l_ref, acc_ref = rest[pp:]
    j = pl.program_id(1)
    rows = q_ref.shape[0]
    page = pages[0].shape[0]
    w = page * nh

    @pl.when(j == 0)
    def _():
        m_ref[...] = jnp.full(m_ref.shape, NEG_INF, F32)
        l_ref[...] = jnp.zeros(l_ref.shape, F32)
        acc_ref[...] = jnp.zeros(acc_ref.shape, F32)

    q = q_ref[...]
    kv = [_flat_kv(pg) for pg in pages]
    s = jnp.concatenate([lax.dot_general(q, k, _NT, preferred_element_type=F32) for k, _ in kv], axis=-1)
    valid = _head_match(rows, w, 2 * sq, nh)
    s = jnp.where(jnp.concatenate([valid] * pp, axis=-1), s, NEG_INF)
    m_old = m_ref[...]
    m_new = jnp.maximum(m_old, jnp.max(s, axis=-1, keepdims=True))
    p = jnp.exp2(s - m_new)
    alpha = jnp.exp2(m_old - m_new)
    l_ref[...] = alpha * l_ref[...] + jnp.sum(p, axis=-1, keepdims=True)
    pb = p.astype(BF16)
    pv = sum(jnp.dot(pb[:, r * w:(r + 1) * w], kv[r][1], preferred_element_type=F32) for r in range(pp))
    acc_ref[...] = alpha * acc_ref[...] + pv
    m_ref[...] = m_new

    @pl.when(j == pl.num_programs(1) - 1)
    def _():
        s_n = lax.dot_general(q, kn_ref[...].astype(BF16), _NT, preferred_element_type=F32)
        s_n = jnp.where(_new_token_mask(rows, sq, 2 * sq, nh), s_n, NEG_INF)
        m_o = m_ref[...]
        m_f = jnp.maximum(m_o, jnp.max(s_n, axis=-1, keepdims=True))
        p_n = jnp.exp2(s_n - m_f)
        a_f = jnp.exp2(m_o - m_f)
        l_f = a_f * l_ref[...] + jnp.sum(p_n, axis=-1, keepdims=True)
        acc = a_f * acc_ref[...] + jnp.dot(p_n.astype(BF16), vn_ref[...].astype(BF16), preferred_element_type=F32)
        o = acc / l_f
        lam = _diff_lambda(lq_ref[...], lam_init)
        for h in range(nh):
            b0 = h * 2 * sq
            d = o[b0:b0 + sq] - lam * o[b0 + sq:b0 + 2 * sq]
            ms = jnp.mean(d * d, axis=-1, keepdims=True)
            o_ref[h * sq:(h + 1) * sq, :] = (d * lax.rsqrt(ms + EPS) * g_ref[...] * (1.0 - lam_init)).astype(o_ref.dtype)


def _head_rows(x, nh):
    db, sq, _ = x.shape
    return x.reshape(db, sq, nh, HEAD_DIM).transpose(0, 2, 1, 3).reshape(db, nh * sq, HEAD_DIM)


def _from_head_rows(o, sq):
    db, rows, _ = o.shape
    nh = rows // sq
    return o.reshape(db, nh, sq, HEAD_DIM).transpose(0, 2, 1, 3).reshape(db, sq, nh * HEAD_DIM)


def _paged_diff(q, kv_new, cache, layer, page_table, diff_lambda, subln_g, lam_init):
    db, sq, w = q.shape
    nh = w // HEAD_DIM
    dd = diff_lambda.shape[-1]
    n_pages = page_table.shape[1]
    page = cache.shape[2]
    pp = max(c for c in (8, 4, 2, 1) if n_pages % c == 0)
    rows = nh * 2 * sq
    q4 = q.reshape(db, sq, nh, HEAD_DIM).transpose(0, 2, 1, 3) * ((dd ** -0.5) * LOG2E)
    lane = jnp.arange(HEAD_DIM) < dd
    q_rows = jnp.stack([jnp.where(lane, q4, 0.0), jnp.where(lane, 0.0, q4)], axis=2)
    q_rows = q_rows.reshape(db, rows, HEAD_DIM).astype(BF16)
    kn = kv_new[..., :w].reshape(db, sq * nh, HEAD_DIM)
    vn = kv_new[..., w:].reshape(db, sq * nh, HEAD_DIM)
    new_spec = pl.BlockSpec((None, sq * nh, HEAD_DIM), lambda bi, j, pt: (bi, 0, 0))
    grid_spec = pltpu.PrefetchScalarGridSpec(
        num_scalar_prefetch=1,
        grid=(db, n_pages // pp),
        in_specs=[pl.BlockSpec((None, rows, HEAD_DIM), lambda bi, j, pt: (bi, 0, 0)), new_spec, new_spec,
                  pl.BlockSpec(diff_lambda.shape, lambda bi, j, pt: (0, 0)),
                  pl.BlockSpec((1, HEAD_DIM), lambda bi, j, pt: (0, 0))] + _page_specs(layer, pp, page, nh),
        out_specs=pl.BlockSpec((None, nh * sq, HEAD_DIM), lambda bi, j, pt: (bi, 0, 0)),
        scratch_shapes=[pltpu.VMEM((rows, 1), F32), pltpu.VMEM((rows, 1), F32), pltpu.VMEM((rows, HEAD_DIM), F32)],
    )
    o = pl.pallas_call(
        functools.partial(_paged_diff_kernel, nh=nh, pp=pp, sq=sq, lam_init=lam_init),
        grid_spec=grid_spec,
        out_shape=jax.ShapeDtypeStruct((db, nh * sq, HEAD_DIM), BF16),
        compiler_params=_params(2),
        name="diff_paged",
    )(page_table, q_rows, kn, vn, diff_lambda, subln_g.reshape(1, HEAD_DIM), *([cache] * pp))
    return _from_head_rows(o, sq)


def _paged_moba_kernel(pt_ref, q_ref, qf_ref, kn_ref, vn_ref, *rest, nh, bps, sq):
    pages = rest[:2 * bps]
    o_ref, m_ref, l_ref, o_acc_ref, g_ref = rest[2 * bps:]
    j = pl.program_id(1)
    nb = m_ref.shape[0]
    rows = q_ref.shape[0]
    page = pages[0].shape[0]
    w = page * nh
    q = q_ref[...]
    qf = qf_ref[...]
    valid = _head_match(rows, w, sq, nh)
    valid = jnp.concatenate([valid, valid], axis=-1)

    for blk in range(bps):
        n = j * bps + blk
        p0, p1 = pages[2 * blk], pages[2 * blk + 1]
        kmean = (jnp.sum(p0[:, 0], axis=0) + jnp.sum(p1[:, 0], axis=0)) * (1.0 / (2 * page))
        kmean_rows = jnp.concatenate([jnp.broadcast_to(kmean[h:h + 1], (sq, HEAD_DIM)) for h in range(nh)], axis=0)
        g_ref[n] = jnp.sum(qf * kmean_rows, axis=-1, keepdims=True)
        k0, v0 = _flat_kv(p0)
        k1, v1 = _flat_kv(p1)
        s = jnp.concatenate([lax.dot_general(q, k0, _NT, preferred_element_type=F32),
                             lax.dot_general(q, k1, _NT, preferred_element_type=F32)], axis=-1)
        s = jnp.where(valid, s, NEG_INF)
        m = jnp.max(s, axis=-1, keepdims=True)
        p = jnp.exp2(s - m)
        m_ref[n] = m
        l_ref[n] = jnp.sum(p, axis=-1, keepdims=True)
        pb = p.astype(BF16)
        o_acc_ref[n] = (jnp.dot(pb[:, :w], v0, preferred_element_type=F32)
                        + jnp.dot(pb[:, w:], v1, preferred_element_type=F32))

    @pl.when(j == pl.num_programs(1) - 1)
    def _():
        n_iota = lax.broadcasted_iota(I32, (nb, rows, 1), 0)
        sel = _select_topk_lowest_index(g_ref[...], n_iota, nb, MOBA_TOPK, 0)
        m_blk = jnp.where(sel, m_ref[...], NEG_INF)
        s_n = lax.dot_general(q, kn_ref[...].astype(BF16), _NT, preferred_element_type=F32)
        s_n = jnp.where(_new_token_mask(rows, sq, sq, nh), s_n, NEG_INF)
        m_all = jnp.maximum(jnp.max(m_blk, axis=0), jnp.max(s_n, axis=-1, keepdims=True))
        p_n = jnp.exp2(s_n - m_all)
        e_blk = jnp.where(sel, jnp.exp2(m_blk - m_all[None]), 0.0)
        l_all = jnp.sum(e_blk * l_ref[...], axis=0) + jnp.sum(p_n, axis=-1, keepdims=True)
        o_all = (jnp.sum(e_blk * o_acc_ref[...], axis=0)
                 + jnp.dot(p_n.astype(BF16), vn_ref[...].astype(BF16), preferred_element_type=F32))
        o_ref[...] = (o_all / l_all).astype(o_ref.dtype)


def _paged_moba(q, kv_new, cache, layer, page_table):
    db, sq, w = q.shape
    nh = w // HEAD_DIM
    n_pages = page_table.shape[1]
    page = cache.shape[2]
    assert 2 * page == MOBA_BLOCK and n_pages % 2 == 0 and sq <= MOBA_BLOCK
    nb = n_pages // 2
    bps = max(c for c in (4, 2, 1) if nb % c == 0)
    pp = 2 * bps
    rows = nh * sq
    qf = _head_rows(q, nh)
    q_rows = (qf * ((HEAD_DIM ** -0.5) * LOG2E)).astype(BF16)
    kn = kv_new[..., :w].reshape(db, sq * nh, HEAD_DIM)
    vn = kv_new[..., w:].reshape(db, sq * nh, HEAD_DIM)
    row_spec = pl.BlockSpec((None, rows, HEAD_DIM), lambda bi, j, pt: (bi, 0, 0))
    new_spec = pl.BlockSpec((None, sq * nh, HEAD_DIM), lambda bi, j, pt: (bi, 0, 0))
    grid_spec = pltpu.PrefetchScalarGridSpec(
        num_scalar_prefetch=1,
        grid=(db, nb // bps),
        in_specs=[row_spec, row_spec, new_spec, new_spec] + _page_specs(layer, pp, page, nh),
        out_specs=row_spec,
        scratch_shapes=[pltpu.VMEM((nb, rows, 1), F32), pltpu.VMEM((nb, rows, 1), F32),
                        pltpu.VMEM((nb, rows, HEAD_DIM), F32), pltpu.VMEM((nb, rows, 1), F32)],
    )
    o = pl.pallas_call(
        functools.partial(_paged_moba_kernel, nh=nh, bps=bps, sq=sq),
        grid_spec=grid_spec,
        out_shape=jax.ShapeDtypeStruct((db, rows, HEAD_DIM), BF16),
        compiler_params=_params(2),
        name="moba_paged",
    )(page_table, q_rows, qf, kn, vn, *([cache] * pp))
    return _from_head_rows(o, sq)


def _shift_rows(x, d, fill):
    row = lax.broadcasted_iota(I32, x.shape, 0)
    return jnp.where(row >= d, pltpu.roll(x, d, 0), fill)


def _rglru_kernel(x_ref, h0_ref, c0_ref, cw_ref, cb_ref, wa_ref, ba_ref, wx_ref, bx_ref, lam_ref,
                  o_ref, hl_ref, cl_ref, hc_ref, tail_ref, *, cw_len):
    c = pl.program_id(2)
    nc = pl.num_programs(2)
    x = x_ref[...]
    tc = x.shape[0]
    nt = cw_len - 1

    @pl.when(c == 0)
    def _():
        hc_ref[...] = h0_ref[...]
        tail_ref[...] = c0_ref[...]

    tail = tail_ref[...]
    row = lax.broadcasted_iota(I32, x.shape, 0)
    cw = cw_ref[...]
    u = cb_ref[...] + x * cw[nt:nt + 1]
    for d in range(1, cw_len):
        fill = jnp.zeros_like(x)
        for r in range(d):
            fill = jnp.where(row == r, tail[nt - d + r:nt - d + r + 1], fill)
        u = u + _shift_rows(x, d, fill) * cw[nt - d:nt - d + 1]

    ub = u.astype(BF16)
    r_g = jax.nn.sigmoid(jnp.dot(ub, wa_ref[...].astype(BF16), preferred_element_type=F32) + ba_ref[...])
    i_g = jax.nn.sigmoid(jnp.dot(ub, wx_ref[...].astype(BF16), preferred_element_type=F32) + bx_ref[...])
    lam = lam_ref[...]
    log_sig = jnp.minimum(lam, 0.0) - jnp.log1p(jnp.exp(-jnp.abs(lam)))
    log_a = RGLRU_C * r_g * log_sig
    a = jnp.exp(log_a)
    bv = jnp.sqrt(-jnp.tanh(log_a) * (a * a + 1.0)) * (i_g * u)

    d = 1
    while d < tc:
        a_prev = _shift_rows(a, d, jnp.ones_like(a))
        b_prev = _shift_rows(bv, d, jnp.zeros_like(bv))
        bv = a * b_prev + bv
        a = a * a_prev
        d *= 2
    h = a * hc_ref[...] + bv
    o_ref[...] = h.astype(o_ref.dtype)
    hc_ref[...] = h[tc - 1:tc]
    if tc >= nt:
        tail_ref[...] = x[tc - nt:tc]
    else:
        tail_ref[...] = jnp.concatenate([tail[tc:], x], axis=0)

    @pl.when(c == nc - 1)
    def _():
        hl_ref[...] = hc_ref[...]
        cl_ref[...] = tail_ref[...]


def _rglru(xr, h0, conv0, conv_w, conv_b, w_a, b_a, w_x, b_x, lam):
    b, s, w = xr.shape
    nblk, bw, _ = w_a.shape
    assert bw == LANES
    cw_len = conv_w.shape[0]
    tc = _tile(s, 256)
    vec = lambda a: a.reshape(1, w)
    vspec = pl.BlockSpec((1, bw), lambda bi, n, c: (0, n))
    wspec = pl.BlockSpec((None, bw, bw), lambda bi, n, c: (n, 0, 0))
    return pl.pallas_call(
        functools.partial(_rglru_kernel, cw_len=cw_len),
        grid=(b, nblk, s // tc),
        in_specs=[pl.BlockSpec((None, tc, bw), lambda bi, n, c: (bi, c, n)),
                  pl.BlockSpec((None, 1, bw), lambda bi, n, c: (bi, 0, n)),
                  pl.BlockSpec((None, cw_len - 1, bw), lambda bi, n, c: (bi, 0, n)),
                  pl.BlockSpec((cw_len, bw), lambda bi, n, c: (0, n)),
                  vspec, wspec, vspec, wspec, vspec, vspec],
        out_specs=[pl.BlockSpec((None, tc, bw), lambda bi, n, c: (bi, c, n)),
                   pl.BlockSpec((None, 1, bw), lambda bi, n, c: (bi, 0, n)),
                   pl.BlockSpec((None, cw_len - 1, bw), lambda bi, n, c: (bi, 0, n))],
        out_shape=[jax.ShapeDtypeStruct((b, s, w), BF16),
                   jax.ShapeDtypeStruct((b, 1, w), F32),
                   jax.ShapeDtypeStruct((b, cw_len - 1, w), F32)],
        scratch_shapes=[pltpu.VMEM((1, bw), F32), pltpu.VMEM((cw_len - 1, bw), F32)],
        compiler_params=_params(3),
        name="rglru",
    )(xr, h0.reshape(b, 1, w), conv0, conv_w, vec(conv_b), w_a, vec(b_a), w_x, vec(b_x), vec(lam))


def _merge_kernel(a0_ref, a1_ref, a2_ref, g0_ref, g1_ref, g2_ref, w_ref, o_ref, wb_ref):
    @pl.when((pl.program_id(1) == 0) & (pl.program_id(2) == 0))
    def _():
        wb_ref[...] = w_ref[...].astype(BF16)

    acc = None
    for n, (a_ref, g_ref) in enumerate(((a0_ref, g0_ref), (a1_ref, g1_ref), (a2_ref, g2_ref))):
        y = jnp.dot(a_ref[...], wb_ref[n], preferred_element_type=F32) * jax.nn.sigmoid(g_ref[...])
        acc = y if acc is None else acc + y
    o_ref[...] = acc.astype(o_ref.dtype)


def _merge(oa, od, orr, gl, w_branch, layer):
    b, s, bw = oa.shape
    d = w_branch.shape[-1]
    tm = _tile(s, 512)
    tn = _tile(d, 512)
    nt = d // tn
    a_spec = pl.BlockSpec((None, tm, bw), lambda j, bi, i: (bi, i, 0))
    g_specs = [pl.BlockSpec((None, tm, tn), functools.partial(lambda j, bi, i, n: (bi, i, n * nt + j), n=n))
               for n in range(3)]
    return pl.pallas_call(
        _merge_kernel,
        grid=(nt, b, s // tm),
        in_specs=[a_spec, a_spec, a_spec] + g_specs
                 + [pl.BlockSpec((None, 3, bw, tn), lambda j, bi, i: (layer, 0, 0, j))],
        out_specs=pl.BlockSpec((None, tm, tn), lambda j, bi, i: (bi, i, j)),
        out_shape=jax.ShapeDtypeStruct((b, s, d), BF16),
        scratch_shapes=[pltpu.VMEM((3, bw, tn), BF16)],
        compiler_params=_params(3),
        name="branch_merge",
    )(oa, od, orr, gl, gl, gl, w_branch)


def _outproj_kernel(a_ref, w_ref, x_ref, g_ref, o_ref, wb_ref):
    @pl.when((pl.program_id(1) == 0) & (pl.program_id(2) == 0))
    def _():
        wb_ref[...] = w_ref[...].astype(BF16)

    y = jnp.dot(a_ref[...], wb_ref[...], preferred_element_type=F32)
    o_ref[...] = x_ref[...] + g_ref[...] * y


def _outproj(merged, w_o, layer, x, g1):
    b, s, d = x.shape
    tm = _tile(s, ROWS_PLAIN)
    tn = _tile(d, 1024)
    return pl.pallas_call(
        _outproj_kernel,
        grid=(d // tn, b, s // tm),
        in_specs=[pl.BlockSpec((None, tm, d), lambda j, bi, i: (bi, i, 0)),
                  pl.BlockSpec((None, d, tn), lambda j, bi, i: (layer, 0, j)),
                  pl.BlockSpec((None, tm, tn), lambda j, bi, i: (bi, i, j)),
                  pl.BlockSpec((None, 1, tn), lambda j, bi, i: (bi, 0, j))],
        out_specs=pl.BlockSpec((None, tm, tn), lambda j, bi, i: (bi, i, j)),
        out_shape=jax.ShapeDtypeStruct((b, s, d), F32),
        scratch_shapes=[pltpu.VMEM((d, tn), BF16)],
        compiler_params=_params(3),
        name="outproj",
    )(merged, w_o, x, g1)


def _stack_rows(rows):
    n = rows[0].shape[1]
    r_iota = lax.broadcasted_iota(I32, (len(rows), n), 0)
    out = jnp.zeros((len(rows), n), rows[0].dtype)
    for k, row in enumerate(rows):
        out = jnp.where(r_iota == k, row, out)
    return out


def _route_kernel(lg_ref, bias_ref, e_ref, w_ref, rank_ref, cnt_ref, carry_ref, *, n_exp):
    i = pl.program_id(0)

    @pl.when(i == 0)
    def _():
        carry_ref[...] = jnp.zeros(carry_ref.shape, F32)

    tile = lg_ref.shape[1]
    pg = n_exp // N_GROUPS
    scores = jax.nn.sigmoid(lg_ref[...])
    biased = scores + bias_ref[...]
    sc3 = scores.reshape(N_GROUPS, pg, tile)
    b3 = biased.reshape(N_GROUPS, pg, tile)
    j_iota = lax.broadcasted_iota(I32, b3.shape, 1)
    eid = lax.broadcasted_iota(I32, b3.shape, 0) * pg + j_iota

    m1 = jnp.max(b3, axis=1, keepdims=True)
    f1 = jnp.min(jnp.where(b3 == m1, j_iota, pg), axis=1, keepdims=True)
    m2 = jnp.max(jnp.where(j_iota == f1, NEG_INF, b3), axis=1, keepdims=True)
    grp = m1 + m2
    g_iota = lax.broadcasted_iota(I32, grp.shape, 0)
    g_sel = _select_topk_lowest_index(grp, g_iota, N_GROUPS, TOPK_GROUPS, 0)
    work = jnp.where(g_sel, b3, NEG_INF)

    e_rows, w_rows = [], []
    multi_hot = jnp.zeros(b3.shape, F32)
    for _ in range(TOP_K):
        m = jnp.max(jnp.max(work, axis=1, keepdims=True), axis=0, keepdims=True)
        first = jnp.min(jnp.min(jnp.where(work == m, eid, n_exp), axis=1, keepdims=True), axis=0, keepdims=True)
        pick = eid == first
        w_k = jnp.sum(jnp.sum(jnp.where(pick, sc3, 0.0), axis=1, keepdims=True), axis=0, keepdims=True)
        work = jnp.where(pick, NEG_INF, work)
        multi_hot = multi_hot + pick.astype(F32)
        e_rows.append(first.reshape(1, tile))
        w_rows.append(w_k.reshape(1, tile))
    e_idx = _stack_rows(e_rows)
    w = _stack_rows(w_rows)
    w = w / jnp.sum(w, axis=0, keepdims=True) * ROUTED_SCALE

    mh = multi_hot.reshape(n_exp, tile)
    r_i = lax.broadcasted_iota(I32, (tile, tile), 0)
    c_i = lax.broadcasted_iota(I32, (tile, tile), 1)
    before = (r_i < c_i).astype(BF16)
    base = carry_ref[:, 0:1] + jnp.dot(mh.astype(BF16), before, preferred_element_type=F32)
    e_sub = lax.broadcasted_iota(I32, (n_exp, tile), 0)
    ranks = [jnp.sum(jnp.where(e_sub == e_rows[k], base, 0.0), axis=0, keepdims=True) for k in range(TOP_K)]
    e_ref[...] = e_idx
    w_ref[...] = w
    rank_ref[...] = _stack_rows(ranks).astype(I32)
    carry_ref[...] = carry_ref[...] + jnp.sum(mh, axis=1, keepdims=True)
    cnt_ref[...] = carry_ref[...]


def _route(logits_t, bias):
    n_exp, t = logits_t.shape
    tile = 256
    out_spec = pl.BlockSpec((TOP_K, tile), lambda i: (0, i))
    return pl.pallas_call(
        functools.partial(_route_kernel, n_exp=n_exp),
        grid=(t // tile,),
        in_specs=[pl.BlockSpec((n_exp, tile), lambda i: (0, i)),
                  pl.BlockSpec((n_exp, 1), lambda i: (0, 0))],
        out_specs=[out_spec, out_spec, out_spec, pl.BlockSpec((n_exp, LANES), lambda i: (0, 0))],
        out_shape=[jax.ShapeDtypeStruct((TOP_K, t), I32), jax.ShapeDtypeStruct((TOP_K, t), F32),
                   jax.ShapeDtypeStruct((TOP_K, t), I32), jax.ShapeDtypeStruct((n_exp, LANES), F32)],
        scratch_shapes=[pltpu.VMEM((n_exp, LANES), F32)],
        compiler_params=_params(1),
        name="route",
    )(logits_t, bias.reshape(n_exp, 1))


def _slot_kernel(e_ref, rank_ref, ps_ref, o_ref):
    e_idx = e_ref[...]
    n_exp = ps_ref.shape[0]
    tile = e_idx.shape[1]
    e_sub = lax.broadcasted_iota(I32, (n_exp, tile), 0)
    ps = ps_ref[...]
    rows = [jnp.sum(jnp.where(e_sub == e_idx[k:k + 1], ps, 0), axis=0, keepdims=True) for k in range(TOP_K)]
    o_ref[...] = _stack_rows(rows) + rank_ref[...]


def _slots(e_idx, rank, p_start):
    _, t = e_idx.shape
    n_exp = p_start.shape[0]
    tile = 256
    spec = pl.BlockSpec((TOP_K, tile), lambda i: (0, i))
    return pl.pallas_call(
        _slot_kernel,
        grid=(t // tile,),
        in_specs=[spec, spec, pl.BlockSpec((n_exp, 1), lambda i: (0, 0))],
        out_specs=spec,
        out_shape=jax.ShapeDtypeStruct((TOP_K, t), I32),
        compiler_params=_params(1),
        name="slots",
    )(e_idx, rank, p_start.reshape(n_exp, 1))


def _zero_tails_kernel(blk_ref, o_ref):
    o_ref[...] = jnp.zeros(o_ref.shape, o_ref.dtype)


def _zero_tails(last_block, n_slots, dw, tm):
    n_exp = last_block.shape[0]
    grid_spec = pltpu.PrefetchScalarGridSpec(
        num_scalar_prefetch=1, grid=(n_exp,), in_specs=[],
        out_specs=pl.BlockSpec((tm, dw), lambda e, blk: (blk[e], 0)))
    return pl.pallas_call(
        _zero_tails_kernel, grid_spec=grid_spec,
        out_shape=jax.ShapeDtypeStruct((n_slots, dw), U32),
        compiler_params=_params(1),
        name="zero_tails",
    )(last_block)


def _row_copy(src_ref, src_row, dst_ref, dst_row, sem):
    return pltpu.make_async_copy(src_ref.at[pl.ds(src_row, 1), :], dst_ref.at[pl.ds(dst_row, 1), :], sem)


def _dispatch_kernel(pos_ref, h_ref, xs_in_ref, xs_ref, sem):
    del xs_in_ref
    tile = h_ref.shape[0]

    def issue(t, _):
        for k in range(TOP_K):
            _row_copy(h_ref, t, xs_ref, pos_ref[k, t], sem).start()
        return 0

    lax.fori_loop(0, tile, issue, 0)

    def drain(t, _):
        for k in range(TOP_K):
            _row_copy(h_ref, 0, xs_ref, 0, sem).wait()
        return 0

    lax.fori_loop(0, tile, drain, 0)


def _dispatch(pos_t, h_all, xs_init):
    t, dw = h_all.shape
    tile = 256
    return pl.pallas_call(
        _dispatch_kernel,
        grid=(t // tile,),
        in_specs=[pl.BlockSpec((TOP_K, tile), lambda i: (0, i), memory_space=pltpu.SMEM),
                  pl.BlockSpec((tile, dw), lambda i: (i, 0)),
                  pl.BlockSpec(memory_space=pl.ANY)],
        out_specs=pl.BlockSpec(memory_space=pl.ANY),
        out_shape=jax.ShapeDtypeStruct(xs_init.shape, xs_init.dtype),
        scratch_shapes=[pltpu.SemaphoreType.DMA],
        input_output_aliases={2: 0},
        compiler_params=_params(1),
        name="dispatch",
    )(pos_t, h_all, xs_init)


def _swiglu_packed(x_packed, wg_ref, wu_ref, wd_ref):
    lo, hi = _unpack_pair(x_packed)
    xl, xh = lo.astype(BF16), hi.astype(BF16)
    hw = xl.shape[1]
    g = (jnp.dot(xl, wg_ref[:hw], preferred_element_type=F32) + jnp.dot(xh, wg_ref[hw:], preferred_element_type=F32))
    u = (jnp.dot(xl, wu_ref[:hw], preferred_element_type=F32) + jnp.dot(xh, wu_ref[hw:], preferred_element_type=F32))
    mid = (g * jax.nn.sigmoid(g) * u).astype(BF16)
    return (jnp.dot(mid, wd_ref[:, :hw], preferred_element_type=F32),
            jnp.dot(mid, wd_ref[:, hw:], preferred_element_type=F32))


def _experts_kernel(be_ref, nu_ref, nxt_ref, slot_ref, x_ref, wg_ref, wu_ref, wd_ref, o_ref,
                    wgf_ref, wuf_ref, wdf_ref, wgb_ref, wub_ref, wdb_ref, sem, *, layer):
    b = pl.program_id(0)
    e = be_ref[b]
    new_expert = (b == 0) | (e != be_ref[jnp.maximum(b - 1, 0)])
    active = b < nu_ref[0]
    slot = slot_ref[e]

    def weight_copies(expert, s):
        return [pltpu.make_async_copy(w_ref.at[layer, expert], f_ref.at[s], sem.at[s, i])
                for i, (w_ref, f_ref) in enumerate(((wg_ref, wgf_ref), (wu_ref, wuf_ref), (wd_ref, wdf_ref)))]

    @pl.when(active & (b == 0))
    def _():
        for c in weight_copies(e, slot):
            c.start()

    @pl.when(active & new_expert)
    def _():
        for c in weight_copies(e, slot):
            c.wait()
        wgb_ref[...] = wgf_ref[slot].astype(BF16)
        wub_ref[...] = wuf_ref[slot].astype(BF16)
        wdb_ref[...] = wdf_ref[slot].astype(BF16)
        nxt = nxt_ref[e]

        @pl.when(nxt >= 0)
        def _():
            for c in weight_copies(nxt, 1 - slot):
                c.start()

    @pl.when(active)
    def _():
        y_lo, y_hi = _swiglu_packed(x_ref[...], wgb_ref, wub_ref, wdb_ref)
        o_ref[...] = _pack_pair(y_lo, y_hi)


def _experts(block_e, n_used, next_expert, slot_of_expert, xs, w_gate, w_up, w_down, layer, tm):
    n_slots, dw = xs.shape
    d, de = w_gate.shape[-2:]
    n_blocks = n_slots // tm

    def row_map(b, be, nu, nxt, slot):
        return (jnp.minimum(b, nu[0] - 1), 0)

    any_spec = pl.BlockSpec(memory_space=pl.ANY)
    grid_spec = pltpu.PrefetchScalarGridSpec(
        num_scalar_prefetch=4,
        grid=(n_blocks,),
        in_specs=[pl.BlockSpec((tm, dw), row_map), any_spec, any_spec, any_spec],
        out_specs=pl.BlockSpec((tm, dw), row_map),
        scratch_shapes=[pltpu.VMEM((2, d, de), F32), pltpu.VMEM((2, d, de), F32), pltpu.VMEM((2, de, d), F32),
                        pltpu.VMEM((d, de), BF16), pltpu.VMEM((d, de), BF16), pltpu.VMEM((de, d), BF16),
                        pltpu.SemaphoreType.DMA((2, 3))],
    )
    return pl.pallas_call(
        functools.partial(_experts_kernel, layer=layer), grid_spec=grid_spec,
        out_shape=jax.ShapeDtypeStruct((n_slots, dw), U32),
        compiler_params=_params(1),
        name="experts",
    )(block_e, n_used, next_expert, slot_of_expert, xs, w_gate, w_up, w_down)


def _combine_kernel(pos_ref, h_ref, x_ref, g_ref, w_ref, sg_ref, su_ref, sd_ref, ys_ref, o_ref,
                    buf_ref, sgb_ref, sub_ref, sdb_ref, sem):
    tile, hw = h_ref.shape

    @pl.when(pl.program_id(0) == 0)
    def _():
        sgb_ref[...] = sg_ref[...].astype(BF16)
        sub_ref[...] = su_ref[...].astype(BF16)
        sdb_ref[...] = sd_ref[...].astype(BF16)

    def issue(t, _):
        for k in range(TOP_K):
            _row_copy(ys_ref, pos_ref[k, t], buf_ref.at[k], t, sem).start()
        return 0

    lax.fori_loop(0, tile, issue, 0)

    acc_lo, acc_hi = _swiglu_packed(h_ref[...], sgb_ref, sub_ref, sdb_ref)

    def drain(t, _):
        for k in range(TOP_K):
            _row_copy(ys_ref, 0, buf_ref.at[k], 0, sem).wait()
        return 0

    lax.fori_loop(0, tile, drain, 0)

    w = w_ref[...]
    for k in range(TOP_K):
        lo, hi = _unpack_pair(buf_ref[k])
        acc_lo = acc_lo + w[:, k:k + 1] * lo
        acc_hi = acc_hi + w[:, k:k + 1] * hi
    g = g_ref[...]
    o_ref[:, :hw] = x_ref[:, :hw] + g[:, :hw] * acc_lo
    o_ref[:, hw:] = x_ref[:, hw:] + g[:, hw:] * acc_hi


def _combine(pos_t, h_all, row0, x, g2, w_tok, ys, s_gate, s_up, s_down, layer):
    t, d = x.shape
    dw = h_all.shape[1]
    de = s_gate.shape[-1]
    tile = 128
    b0 = row0 // tile
    g_rows = g2.shape[1]
    return pl.pallas_call(
        _combine_kernel,
        grid=(t // tile,),
        in_specs=[pl.BlockSpec((TOP_K, tile), lambda i: (0, b0 + i), memory_space=pltpu.SMEM),
                  pl.BlockSpec((tile, dw), lambda i: (b0 + i, 0)),
                  pl.BlockSpec((tile, d), lambda i: (i, 0)),
                  pl.BlockSpec((None, g_rows, d), lambda i: (i, 0, 0)),
                  pl.BlockSpec((tile, TOP_K), lambda i: (b0 + i, 0)),
                  pl.BlockSpec((None, d, de), lambda i: (layer, 0, 0)),
                  pl.BlockSpec((None, d, de), lambda i: (layer, 0, 0)),
                  pl.BlockSpec((None, de, d), lambda i: (layer, 0, 0)),
                  pl.BlockSpec(memory_space=pl.ANY)],
        out_specs=pl.BlockSpec((tile, d), lambda i: (i, 0)),
        out_shape=jax.ShapeDtypeStruct((t, d), F32),
        scratch_shapes=[pltpu.VMEM((TOP_K, tile, dw), U32), pltpu.VMEM((d, de), BF16),
                        pltpu.VMEM((d, de), BF16), pltpu.VMEM((de, d), BF16), pltpu.SemaphoreType.DMA],
        compiler_params=_params(1),
        name="combine",
    )(pos_t, h_all, x, g2, w_tok, s_gate, s_up, s_down, ys)


def _mixers(x, mods, layer, p, lam_init, past):
    b, s, d = x.shape
    w = p["moba_q_norm_g"].shape[-1] * (d // 256)
    sh1, sc1, g1 = mods
    h = _norm1(x, p["norm1_g"][layer], sc1, sh1)
    nh = w // HEAD_DIM
    tile_gain = lambda g: jnp.tile(g.reshape(1, -1), (1, nh))
    w_in = p["w_in"]
    tn = w
    prompt = past is None
    dd = p["diff_dim"]
    flat = (lambda a: a) if prompt else (lambda a: a.reshape(1, b * s, a.shape[-1]))
    unflat = (lambda a: a) if prompt else (lambda a: a.reshape(b, s, a.shape[-1]))
    hf = flat(h)
    qa = unflat(_inproj(hf, w_in, layer, 0, w, tn, F32, tile_gain(p["moba_q_norm_g"][layer]), LANES, 1))
    kv_a = unflat(_inproj(hf, w_in, layer, w, 2 * w, tn, F32, tile_gain(p["moba_k_norm_g"][layer]), LANES, 1, prompt))
    qd = unflat(_inproj(hf, w_in, layer, 3 * w, w, tn, F32, tile_gain(p["diff_q_norm_g"][layer]), dd, 1))
    kv_d = unflat(_inproj(hf, w_in, layer, 4 * w, 2 * w, tn, F32, tile_gain(p["diff_k_norm_g"][layer]), dd, 1, prompt))
    xr = unflat(_inproj(hf, w_in, layer, 6 * w, w, tn, F32))
    gl = unflat(_inproj(hf, w_in, layer, 7 * w, 3 * d, tn, F32))

    if prompt:
        kv_a, kvb_a = kv_a
        kv_d, kvb_d = kv_d
        oa = _moba_prompt(qa, kvb_a, _kmean(kv_a, w))
        od = _diff_prompt(qd, kvb_d, p["diff_lambda"][layer], p["diff_subln_g"][layer], lam_init)
        h0 = jnp.zeros((b, w), F32)
        conv0 = jnp.zeros((b, p["rglru_conv_w"].shape[1] - 1, w), F32)
    else:
        cache_a, cache_d, page_table, h0, conv0 = past
        oa = _paged_moba(qa, kv_a, cache_a, layer, page_table)
        od = _paged_diff(qd, kv_d, cache_d, layer, page_table, p["diff_lambda"][layer],
                         p["diff_subln_g"][layer], lam_init)
    orr, h_last, conv_last = _rglru(xr, h0, conv0, p["rglru_conv_w"][layer], p["rglru_conv_b"][layer],
                                    p["rglru_wa"][layer], p["rglru_ba"][layer], p["rglru_wx"][layer],
                                    p["rglru_bx"][layer], p["rglru_lambda"][layer])
    merged = unflat(_merge(flat(oa), flat(od), flat(orr), flat(gl), p["w_branch"], layer))
    x1 = _outproj(merged, p["w_o"], layer, x, g1)
    return x1, kv_a, kv_d, h_last.reshape(b, w), conv_last


def _moe(x1p, x1s, mods_p, mods_s, layer, p):
    bp, sp, d = x1p.shape
    bs, ss, _ = x1s.shape
    n_exp = p["router_w"].shape[-1]
    tp, ts = bp * sp, bs * ss
    dw = d // 2
    h2p, lgp = _norm2_router(x1p, p["norm2_g"][layer], mods_p[1], mods_p[0], p["router_w"][layer])
    h2s, lgs = _norm2_router(x1s, p["norm2_g"][layer], mods_s[1], mods_s[0], p["router_w"][layer])
    t_all = tp + -(-ts // 256) * 256
    pad = t_all - tp - ts
    h_all = jnp.concatenate([h2p.reshape(tp, dw), h2s.reshape(ts, dw), jnp.zeros((pad, dw), U32)], axis=0)
    lg_all = jnp.concatenate([lgp.reshape(tp, n_exp), lgs.reshape(ts, n_exp), jnp.zeros((pad, n_exp), F32)], axis=0)
    e_idx, w_t, rank, cnt = _route(lg_all.T, p["router_bias"][layer])

    tm = 256
    n_blocks = -(-(t_all * TOP_K) // tm) + n_exp
    counts = cnt[:, 0].astype(I32)
    padded = (counts + tm - 1) // tm * tm
    p_end = jnp.cumsum(padded)
    p_start = p_end - padded
    n_used = (p_end[-1:] // tm).astype(I32)
    block_row0 = jnp.arange(n_blocks, dtype=I32) * tm
    block_e = jnp.minimum(jnp.sum((p_end[None, :] <= block_row0[:, None]).astype(I32), axis=1), n_exp - 1)
    last_block = jnp.maximum(p_end // tm - 1, 0).astype(I32)
    pos_t = _slots(e_idx, rank, p_start.astype(I32))

    xs = _dispatch(pos_t, h_all, _zero_tails(last_block, n_blocks * tm, dw, tm))
    e_iota = jnp.arange(n_exp, dtype=I32)
    nonempty = counts > 0
    later = jnp.where(nonempty[None, :] & (e_iota[None, :] > e_iota[:, None]), e_iota[None, :], n_exp)
    next_expert = jnp.min(later, axis=1)
    next_expert = jnp.where(next_expert == n_exp, -1, next_expert).astype(I32)
    slot_of_expert = ((jnp.cumsum(nonempty.astype(I32)) - 1) % 2).astype(I32)
    ys = _experts(block_e, n_used, next_expert, slot_of_expert, xs,
                  p["expert_w_gate"], p["expert_w_up"], p["expert_w_down"], layer, tm)

    w_tok = w_t.T
    tile = 128
    shared = (p["shared_w_gate"], p["shared_w_up"], p["shared_w_down"], layer)
    g2p = jnp.broadcast_to(mods_p[2][:, None], (bp, sp // tile, 1, d)).reshape(tp // tile, 1, d)
    yp = _combine(pos_t, h_all, 0, x1p.reshape(tp, d), g2p, w_tok, ys, *shared)
    ts_pad = -(-ts // tile) * tile
    x1s_pad = jnp.concatenate([x1s.reshape(ts, d), jnp.zeros((ts_pad - ts, d), F32)], axis=0)
    g2s = jnp.broadcast_to(mods_s[2], (bs, ss, d)).reshape(ts, d)
    g2s = jnp.concatenate([g2s, jnp.zeros((ts_pad - ts, d), F32)], axis=0).reshape(ts_pad // tile, tile, d)
    ysamp = _combine(pos_t, h_all, tp, x1s_pad, g2s, w_tok, ys, *shared)
    return yp.reshape(bp, sp, d), ysamp[:ts].reshape(bs, ss, d)


def kernel(x_prompt, x_sample, cache_moba_kv, cache_diff_kv, state_rglru_h, state_rglru_conv, page_table, c_prompt, c_sample, w_ada, b_ada, norm1_g, norm2_g, w_in, moba_q_norm_g, moba_k_norm_g, diff_q_norm_g, diff_k_norm_g, diff_lambda, diff_subln_g, rglru_conv_w, rglru_conv_b, rglru_wa, rglru_ba, rglru_wx, rglru_bx, rglru_lambda, w_branch, w_o, router_w, router_bias, expert_w_gate, expert_w_up, expert_w_down, shared_w_gate, shared_w_up, shared_w_down):
    depth = w_ada.shape[0]
    bp, sp, d = x_prompt.shape
    bs, ss, _ = x_sample.shape
    nh, hd = cache_moba_kv.shape[-2:]
    assert hd == HEAD_DIM and sp % MOBA_BLOCK == 0 and (bp * sp) % 256 == 0
    p = dict(norm1_g=norm1_g, norm2_g=norm2_g, w_in=w_in, moba_q_norm_g=moba_q_norm_g,
             moba_k_norm_g=moba_k_norm_g, diff_dim=diff_q_norm_g.shape[-1],
             diff_q_norm_g=diff_q_norm_g.reshape(depth, -1), diff_k_norm_g=diff_k_norm_g.reshape(depth, -1),
             diff_lambda=diff_lambda, diff_subln_g=diff_subln_g, rglru_conv_w=rglru_conv_w,
             rglru_conv_b=rglru_conv_b, rglru_wa=rglru_wa, rglru_ba=rglru_ba, rglru_wx=rglru_wx,
             rglru_bx=rglru_bx, rglru_lambda=rglru_lambda, w_branch=w_branch, w_o=w_o,
             router_w=router_w, router_bias=router_bias, expert_w_gate=expert_w_gate,
             expert_w_up=expert_w_up, expert_w_down=expert_w_down, shared_w_gate=shared_w_gate,
             shared_w_up=shared_w_up, shared_w_down=shared_w_down)
    n_c = bp + bs
    r = -(-n_c // 8) * 8
    c_all = jnp.concatenate([c_prompt, c_sample, jnp.zeros((r - n_c, d), F32)], axis=0)
    mod = _adaln(c_all, w_ada, b_ada)

    def mods_of(layer, lo, hi):
        m = mod[layer, lo:hi][:, None, :]
        return [m[..., i * d:(i + 1) * d] for i in range(6)]

    yp, ys = x_prompt, x_sample
    outs = [[] for _ in range(8)]
    for layer in range(depth):
        lam_init = 0.8 - 0.6 * math.exp(-0.3 * layer)
        mp = mods_of(layer, 0, bp)
        ms = mods_of(layer, bp, n_c)
        x1p, kva_p, kvd_p, hl_p, cl_p = _mixers(yp, mp[0:3], layer, p, lam_init, None)
        past = (cache_moba_kv, cache_diff_kv, page_table, state_rglru_h[layer], state_rglru_conv[layer])
        x1s, kva_s, kvd_s, hl_s, cl_s = _mixers(ys, ms[0:3], layer, p, lam_init, past)
        yp, ys = _moe(x1p, x1s, mp[3:6], ms[3:6], layer, p)
        for lst, val in zip(outs, (kva_p.reshape(bp, sp, 2, nh, hd), kva_s.reshape(bs, ss, 2, nh, hd),
                                   kvd_p.reshape(bp, sp, 2, nh, hd), kvd_s.reshape(bs, ss, 2, nh, hd),
                                   hl_p, hl_s, cl_p, cl_s)):
            lst.append(val)
    return (yp, ys) + tuple(jnp.stack(o) for o in outs)
```

```python
import functools
import math

import jax
import jax.numpy as jnp
from jax import lax
from jax.experimental import pallas as pl
from jax.experimental.pallas import tpu as pltpu

F32 = jnp.float32
BF16 = jnp.bfloat16
I32 = jnp.int32
U32 = jnp.uint32
EPS = 1e-6
NEG_INF = float("-inf")
LOG2E = 1.4426950408889634

LANES = 128
HEAD_DIM = 128
MOBA_BLOCK = 256
MOBA_TOPK = 3
RGLRU_C = 8.0
TOP_K = 8
N_GROUPS = 8
TOPK_GROUPS = 4
ROUTED_SCALE = 2.5
VMEM_LIMIT_BYTES = 56 * 1024 * 1024
QUERY_LANES = LANES
MOBA_UNROLL = 4
DIFF_UNROLL = 2
ROWS_PLAIN = 1024
ROWS_EPILOGUE = 512

_NT = (((1,), (1,)), ((), ()))
_NN = (((1,), (0,)), ((), ()))


def _params(n_axes):
    return pltpu.CompilerParams(dimension_semantics=("arbitrary",) * n_axes,
                                vmem_limit_bytes=VMEM_LIMIT_BYTES)


def _tile(n, pref):
    return pref if n % pref == 0 else n


def _split_bf16(x):
    hi = x.astype(BF16)
    lo = (x - hi.astype(F32)).astype(BF16)
    return hi, lo


def _dot3(a, b, dims):
    ah, al = _split_bf16(a)
    bh, bl = _split_bf16(b)
    dg = functools.partial(lax.dot_general, dimension_numbers=dims, preferred_element_type=F32)
    return dg(ah, bh) + dg(ah, bl) + dg(al, bh)


def _pack_pair(lo, hi):
    lo_bits = lax.bitcast_convert_type(lo.astype(BF16).astype(F32), U32) >> 16
    hi_bits = lax.bitcast_convert_type(hi.astype(BF16).astype(F32), U32) & jnp.uint32(0xFFFF0000)
    return hi_bits | lo_bits


def _unpack_pair(u):
    lo = lax.bitcast_convert_type(u << 16, F32)
    hi = lax.bitcast_convert_type(u & jnp.uint32(0xFFFF0000), F32)
    return lo, hi


def _group_ms(yc, gs):
    sq = yc * yc
    if gs == LANES:
        return jnp.mean(sq, axis=-1, keepdims=True)
    lane = lax.broadcasted_iota(I32, sq.shape, 1)
    lo = lane < gs
    s_lo = jnp.sum(jnp.where(lo, sq, 0.0), axis=-1, keepdims=True)
    s_hi = jnp.sum(jnp.where(lo, 0.0, sq), axis=-1, keepdims=True)
    return jnp.where(lo, s_lo, s_hi) * (1.0 / gs)


def _adaln_kernel(c_ref, w_ref, b_ref, o_ref):
    c = c_ref[...]
    s = (c * jax.nn.sigmoid(c)).astype(BF16)
    o_ref[...] = jnp.dot(s, w_ref[...].astype(BF16), preferred_element_type=F32) + b_ref[...]


def _adaln(c_all, w_ada, b_ada):
    depth, d, n = w_ada.shape
    r = c_all.shape[0]
    tn = _tile(n, 1024)
    return pl.pallas_call(
        _adaln_kernel,
        grid=(depth, n // tn),
        in_specs=[pl.BlockSpec((r, d), lambda l, j: (0, 0)),
                  pl.BlockSpec((None, d, tn), lambda l, j: (l, 0, j)),
                  pl.BlockSpec((None, 1, tn), lambda l, j: (l, 0, j))],
        out_specs=pl.BlockSpec((None, r, tn), lambda l, j: (l, 0, j)),
        out_shape=jax.ShapeDtypeStruct((depth, r, n), F32),
        compiler_params=_params(2),
        name="adaln",
    )(c_all, w_ada, b_ada.reshape(depth, 1, n))


def _norm_mod(x, g, sc, sh):
    y = x * lax.rsqrt(jnp.mean(x * x, axis=-1, keepdims=True) + EPS) * g
    return y * (1.0 + sc) + sh


def _norm1_kernel(x_ref, g_ref, sc_ref, sh_ref, o_ref):
    o_ref[...] = _norm_mod(x_ref[...], g_ref[...], sc_ref[...], sh_ref[...]).astype(o_ref.dtype)


def _norm1(x, g, sc, sh):
    b, s, d = x.shape
    tm = _tile(s, 512)
    mod_spec = pl.BlockSpec((None, 1, d), lambda bi, i: (bi, 0, 0))
    return pl.pallas_call(
        _norm1_kernel,
        grid=(b, s // tm),
        in_specs=[pl.BlockSpec((None, tm, d), lambda bi, i: (bi, i, 0)),
                  pl.BlockSpec((1, d), lambda bi, i: (0, 0)), mod_spec, mod_spec],
        out_specs=pl.BlockSpec((None, tm, d), lambda bi, i: (bi, i, 0)),
        out_shape=jax.ShapeDtypeStruct((b, s, d), BF16),
        compiler_params=_params(2),
        name="norm1",
    )(x, g.reshape(1, d), sc, sh)


def _norm2_router_kernel(x_ref, g_ref, sc_ref, sh_ref, rw_ref, o_ref, lg_ref):
    h = _norm_mod(x_ref[...], g_ref[...], sc_ref[...], sh_ref[...])
    half = h.shape[1] // 2
    o_ref[...] = _pack_pair(h[:, :half], h[:, half:])
    lg_ref[...] = _dot3(h, rw_ref[...], _NN)


def _norm2_router(x, g, sc, sh, router_w):
    b, s, d = x.shape
    e = router_w.shape[-1]
    tm = _tile(s, 256)
    mod_spec = pl.BlockSpec((None, 1, d), lambda bi, i: (bi, 0, 0))
    return pl.pallas_call(
        _norm2_router_kernel,
        grid=(b, s // tm),
        in_specs=[pl.BlockSpec((None, tm, d), lambda bi, i: (bi, i, 0)),
                  pl.BlockSpec((1, d), lambda bi, i: (0, 0)), mod_spec, mod_spec,
                  pl.BlockSpec((d, e), lambda bi, i: (0, 0))],
        out_specs=[pl.BlockSpec((None, tm, d // 2), lambda bi, i: (bi, i, 0)),
                   pl.BlockSpec((None, tm, e), lambda bi, i: (bi, i, 0))],
        out_shape=[jax.ShapeDtypeStruct((b, s, d // 2), U32), jax.ShapeDtypeStruct((b, s, e), F32)],
        compiler_params=_params(2),
        name="norm2_router",
    )(x, g.reshape(1, d), sc, sh, router_w)


def _inproj_kernel(a_ref, w_ref, g_ref, o_ref, *rest, norm_ntiles, n_tiles, gs):
    ob_ref = rest[0] if len(rest) == 2 else None
    wb_ref = rest[-1]
    j = pl.program_id(0)

    @pl.when((pl.program_id(1) == 0) & (pl.program_id(2) == 0))
    def _():
        wb_ref[...] = w_ref[...].astype(BF16)

    y = jnp.dot(a_ref[...], wb_ref[...], preferred_element_type=F32)

    def write(sl, val):
        o_ref[:, sl] = val.astype(o_ref.dtype)
        if ob_ref is not None:
            ob_ref[:, sl] = val.astype(BF16)

    def write_normed():
        g = g_ref[...]
        for c in range(y.shape[1] // LANES):
            sl = slice(c * LANES, (c + 1) * LANES)
            yc = y[:, sl]
            write(sl, yc * lax.rsqrt(_group_ms(yc, gs) + EPS) * g[:, sl])

    def write_raw():
        write(slice(None), y)

    if norm_ntiles == 0:
        write_raw()
    elif norm_ntiles == n_tiles:
        write_normed()
    else:
        pl.when(j < norm_ntiles)(write_normed)
        pl.when(j >= norm_ntiles)(write_raw)


def _inproj(h, w_in, layer, col0, ncols, tn, out_dtype, gain=None, gs=LANES, norm_ntiles=0, bf16_copy=False):
    b, s, k = h.shape
    tm = _tile(s, ROWS_PLAIN if norm_ntiles == 0 and not bf16_copy else ROWS_EPILOGUE)
    n_tiles = ncols // tn
    cb = col0 // tn
    if gain is None:
        gain = jnp.ones((1, tn), F32)
    out_spec = pl.BlockSpec((None, tm, tn), lambda j, bi, i: (bi, i, j))
    out_shape = jax.ShapeDtypeStruct((b, s, ncols), out_dtype)
    return pl.pallas_call(
        functools.partial(_inproj_kernel, norm_ntiles=norm_ntiles, n_tiles=n_tiles, gs=gs),
        grid=(n_tiles, b, s // tm),
        in_specs=[pl.BlockSpec((None, tm, k), lambda j, bi, i: (bi, i, 0)),
                  pl.BlockSpec((None, k, tn), lambda j, bi, i: (layer, 0, cb + j)),
                  pl.BlockSpec((1, tn), lambda j, bi, i: (0, 0))],
        out_specs=[out_spec, out_spec] if bf16_copy else out_spec,
        out_shape=[out_shape, jax.ShapeDtypeStruct((b, s, ncols), BF16)] if bf16_copy else out_shape,
        scratch_shapes=[pltpu.VMEM((k, tn), BF16)],
        compiler_params=_params(3),
        name="inproj",
    )(h, w_in, gain)


def _kmean_kernel(k_ref, o_ref, *, nb):
    x = k_ref[...]
    o_ref[...] = jnp.mean(x.reshape(nb, MOBA_BLOCK, x.shape[-1]), axis=1)


def _kmean(kv, w):
    b, s, _ = kv.shape
    nf = s // MOBA_BLOCK
    nb = _tile(nf, 8)
    return pl.pallas_call(
        functools.partial(_kmean_kernel, nb=nb),
        grid=(b, nf // nb),
        in_specs=[pl.BlockSpec((None, nb * MOBA_BLOCK, w), lambda bi, i: (bi, i, 0))],
        out_specs=pl.BlockSpec((None, nb, w), lambda bi, i: (bi, i, 0)),
        out_shape=jax.ShapeDtypeStruct((b, nf, w), F32),
        compiler_params=_params(2),
        name="moba_kmean",
    )(kv)


def _select_topk_lowest_index(gate, idx, n_idx, k, axis):
    sel = jnp.zeros(gate.shape, jnp.bool_)
    for _ in range(k):
        m = jnp.max(gate, axis=axis, keepdims=True)
        first = jnp.min(jnp.where(gate == m, idx, n_idx), axis=axis, keepdims=True)
        pick = (idx == first) & (m > NEG_INF)
        sel = sel | pick
        gate = jnp.where(pick, NEG_INF, gate)
    return sel


def _fill_vt(v_ref, vt_ref, tk):
    def body(c, _):
        r0 = pl.multiple_of(c * tk, tk)
        vt_ref[:, pl.ds(r0, tk)] = v_ref[pl.ds(r0, tk), :].astype(F32).T.astype(BF16)
        return 0

    lax.fori_loop(0, v_ref.shape[0] // tk, body, 0)


def _flash_streams(streams, k_ref, vt_ref, n_iters, unroll, n_blocks, tk, mask_fn):
    d = streams[0].shape[0]
    ql = QUERY_LANES

    def body(j, carry):
        carry = list(carry)
        for u in range(unroll):
            n = j * unroll + u
            r0 = pl.multiple_of(jnp.minimum(n, n_blocks - 1) * tk, tk)
            kb = k_ref[pl.ds(r0, tk), :]
            vtb = vt_ref[:, pl.ds(r0, tk)]
            for si, q_t in enumerate(streams):
                m_i, l_i, acc = carry[si]
                s_t = jnp.dot(kb, q_t, preferred_element_type=F32)
                s_t = jnp.where(mask_fn(n, si), s_t, NEG_INF)
                m_new = jnp.maximum(m_i, jnp.max(s_t, axis=0, keepdims=True))
                m_safe = jnp.where(m_new == NEG_INF, 0.0, m_new)
                p = jnp.exp2(s_t - m_safe)
                alpha = jnp.exp2(m_i - m_safe)
                l_new = alpha * l_i + jnp.sum(p, axis=0, keepdims=True)
                acc = alpha * acc + jnp.dot(vtb, p.astype(BF16), preferred_element_type=F32)
                carry[si] = (m_new, l_new, acc)
        return tuple(carry)

    init = tuple((jnp.full((1, ql), NEG_INF, F32), jnp.zeros((1, ql), F32), jnp.zeros((d, ql), F32))
                 for _ in streams)
    out = lax.fori_loop(0, n_iters, body, init)
    return [(acc, l_i) for (_, l_i, acc) in out]


def _moba_prompt_kernel(q_ref, k_ref, v_ref, km_ref, o_ref, vt_ref, sel_ref, *, scale):
    i = pl.program_id(2)
    tk = MOBA_BLOCK
    ql = QUERY_LANES

    @pl.when(i == 0)
    def _():
        _fill_vt(v_ref, vt_ref, tk)

    q = q_ref[...]
    tq = q.shape[0]
    nf = km_ref.shape[0]
    q_t = q.T
    gate = _dot3(km_ref[...], q_t, _NN)
    n_iota = lax.broadcasted_iota(I32, gate.shape, 0)
    gate = jnp.where(n_iota < i, gate, NEG_INF)
    sel = _select_topk_lowest_index(gate, n_iota, nf, MOBA_TOPK, 0) | (n_iota == i)
    sel = sel.astype(F32)
    for r in range(nf):
        sel_ref[r] = sel[r:r + 1, :]
    for r in range(nf, sel_ref.shape[0]):
        sel_ref[r] = jnp.zeros((1, tq), F32)
    qs_t = (q_t * (scale * LOG2E)).astype(BF16)
    kr = lax.broadcasted_iota(I32, (tk, ql), 0)
    qc = lax.broadcasted_iota(I32, (tk, ql), 1)
    streams = [qs_t[:, h * ql:(h + 1) * ql] for h in range(tq // ql)]

    def mask_fn(n, si):
        causal = kr + (n - i) * tk <= qc + si * ql
        return causal & (sel_ref[n][:, si * ql:(si + 1) * ql] > 0.0)

    res = _flash_streams(streams, k_ref, vt_ref, (i + MOBA_UNROLL) // MOBA_UNROLL, MOBA_UNROLL, nf, tk, mask_fn)
    for h, (acc, l_i) in enumerate(res):
        o_ref[h * ql:(h + 1) * ql, :] = (acc / l_i).T.astype(o_ref.dtype)


def _moba_prompt(q, kvb, kmean):
    b, s, w = q.shape
    nh = w // HEAD_DIM
    nf = kmean.shape[1]
    return pl.pallas_call(
        functools.partial(_moba_prompt_kernel, scale=HEAD_DIM ** -0.5),
        grid=(b, nh, s // MOBA_BLOCK),
        in_specs=[pl.BlockSpec((None, MOBA_BLOCK, HEAD_DIM), lambda bi, h, i: (bi, i, h)),
                  pl.BlockSpec((None, s, HEAD_DIM), lambda bi, h, i: (bi, 0, h)),
                  pl.BlockSpec((None, s, HEAD_DIM), lambda bi, h, i: (bi, 0, nh + h)),
                  pl.BlockSpec((None, nf, HEAD_DIM), lambda bi, h, i: (bi, 0, h))],
        out_specs=pl.BlockSpec((None, MOBA_BLOCK, HEAD_DIM), lambda bi, h, i: (bi, i, h)),
        out_shape=jax.ShapeDtypeStruct((b, s, w), BF16),
        scratch_shapes=[pltpu.VMEM((HEAD_DIM, s), BF16), pltpu.VMEM((nf + MOBA_UNROLL, 1, MOBA_BLOCK), F32)],
        compiler_params=_params(3),
        name="moba_prompt",
    )(q, kvb, kvb, kmean)


def _diff_lambda(lq, lam_init):
    a = jnp.sum(lq[0:1] * lq[1:2], axis=-1, keepdims=True)
    b = jnp.sum(lq[2:3] * lq[3:4], axis=-1, keepdims=True)
    return jnp.exp(a) - jnp.exp(b) + lam_init


def _diff_prompt_kernel(q_ref, k_ref, v_ref, lq_ref, g_ref, o_ref, vt_ref, *, dd, lam_init, tk):
    i = pl.program_id(2)
    ql = QUERY_LANES

    @pl.when(i == 0)
    def _():
        _fill_vt(v_ref, vt_ref, tk)

    q = q_ref[...]
    tq = q.shape[0]
    nblk = k_ref.shape[0] // tk
    q_t = q.T * ((dd ** -0.5) * LOG2E)
    dr = lax.broadcasted_iota(I32, q_t.shape, 0)
    maps = [jnp.where(dr < dd, q_t, 0.0).astype(BF16), jnp.where(dr < dd, 0.0, q_t).astype(BF16)]
    lam = _diff_lambda(lq_ref[...], lam_init)
    kr = lax.broadcasted_iota(I32, (tk, ql), 0)
    qc = lax.broadcasted_iota(I32, (tk, ql), 1)
    n_half = tq // ql
    streams = [m_t[:, h * ql:(h + 1) * ql] for h in range(n_half) for m_t in maps]

    def mask_fn(n, si):
        return kr + (n - i) * tk <= qc + (si // 2) * ql

    res = _flash_streams(streams, k_ref, vt_ref, (i + DIFF_UNROLL) // DIFF_UNROLL, DIFF_UNROLL, nblk, tk, mask_fn)
    for h in range(n_half):
        (a1, l1), (a2, l2) = res[2 * h], res[2 * h + 1]
        dlt = (a1 / l1 - lam * (a2 / l2)).T
        ms = jnp.mean(dlt * dlt, axis=-1, keepdims=True)
        o_ref[h * ql:(h + 1) * ql, :] = (dlt * lax.rsqrt(ms + EPS) * g_ref[...] * (1.0 - lam_init)).astype(o_ref.dtype)


def _diff_prompt(q, kvb, diff_lambda, subln_g, lam_init):
    b, s, w = q.shape
    nh = w // HEAD_DIM
    dd = diff_lambda.shape[-1]
    tq = _tile(s, 256)
    return pl.pallas_call(
        functools.partial(_diff_prompt_kernel, dd=dd, lam_init=lam_init, tk=tq),
        grid=(b, nh, s // tq),
        in_specs=[pl.BlockSpec((None, tq, HEAD_DIM), lambda bi, h, i: (bi, i, h)),
                  pl.BlockSpec((None, s, HEAD_DIM), lambda bi, h, i: (bi, 0, h)),
                  pl.BlockSpec((None, s, HEAD_DIM), lambda bi, h, i: (bi, 0, nh + h)),
                  pl.BlockSpec(diff_lambda.shape, lambda bi, h, i: (0, 0)),
                  pl.BlockSpec((1, HEAD_DIM), lambda bi, h, i: (0, 0))],
        out_specs=pl.BlockSpec((None, tq, HEAD_DIM), lambda bi, h, i: (bi, i, h)),
        out_shape=jax.ShapeDtypeStruct((b, s, w), BF16),
        scratch_shapes=[pltpu.VMEM((HEAD_DIM, s), BF16)],
        compiler_params=_params(3),
        name="diff_prompt",
    )(q, kvb, kvb, diff_lambda, subln_g.reshape(1, HEAD_DIM))


def _page_specs(layer, pp, page, nh):
    return [pl.BlockSpec((None, None, page, 2, nh, HEAD_DIM),
                         functools.partial(lambda bi, j, pt, r: (layer, pt[bi, pp * j + r], 0, 0, 0, 0), r=r))
            for r in range(pp)]


def _flat_kv(pg_ref):
    page, _, nh, d = pg_ref.shape
    k = pg_ref[:, 0].reshape(page * nh, d).astype(BF16)
    v = pg_ref[:, 1].reshape(page * nh, d).astype(BF16)
    return k, v


def _head_match(rows, cols, rows_per_head, nh):
    r = lax.broadcasted_iota(I32, (rows, cols), 0)
    c = lax.broadcasted_iota(I32, (rows, cols), 1)
    return (r // rows_per_head) == (c % nh)


def _new_token_mask(rows, sq, rows_per_head, nh):
    r = lax.broadcasted_iota(I32, (rows, sq * nh), 0)
    c = lax.broadcasted_iota(I32, (rows, sq * nh), 1)
    return _head_match(rows, sq * nh, rows_per_head, nh) & ((c // nh) <= (r % sq))


def _paged_diff_kernel(pt_ref, q_ref, kn_ref, vn_ref, lq_ref, g_ref, *rest, nh, pp, sq, lam_init):
    pages = rest[:pp]
    o_ref, m_ref, l_ref, acc_ref = rest[pp:]
    j = pl.program_id(1)
    rows = q_ref.shape[0]
    page = pages[0].shape[0]
    w = page * nh

    @pl.when(j == 0)
    def _():
        m_ref[...] = jnp.full(m_ref.shape, NEG_INF, F32)
        l_ref[...] = jnp.zeros(l_ref.shape, F32)
        acc_ref[...] = jnp.zeros(acc_ref.shape, F32)

    q = q_ref[...]
    kv = [_flat_kv(pg) for pg in pages]
    s = jnp.concatenate([lax.dot_general(q, k, _NT, preferred_element_type=F32) for k, _ in kv], axis=-1)
    valid = _head_match(rows, w, 2 * sq, nh)
    s = jnp.where(jnp.concatenate([valid] * pp, axis=-1), s, NEG_INF)
    m_old = m_ref[...]
    m_new = jnp.maximum(m_old, jnp.max(s, axis=-1, keepdims=True))
    p = jnp.exp2(s - m_new)
    alpha = jnp.exp2(m_old - m_new)
    l_ref[...] = alpha * l_ref[...] + jnp.sum(p, axis=-1, keepdims=True)
    pb = p.astype(BF16)
    pv = sum(jnp.dot(pb[:, r * w:(r + 1) * w], kv[r][1], preferred_element_type=F32) for r in range(pp))
    acc_ref[...] = alpha * acc_ref[...] + pv
    m_ref[...] = m_new

    @pl.when(j == pl.num_programs(1) - 1)
    def _():
        s_n = lax.dot_general(q, kn_ref[...].astype(BF16), _NT, preferred_element_type=F32)
        s_n = jnp.where(_new_token_mask(rows, sq, 2 * sq, nh), s_n, NEG_INF)
        m_o = m_ref[...]
        m_f = jnp.maximum(m_o, jnp.max(s_n, axis=-1, keepdims=True))
        p_n = jnp.exp2(s_n - m_f)
        a_f = jnp.exp2(m_o - m_f)
        l_f = a_f * l_ref[...] + jnp.sum(p_n, axis=-1, keepdims=True)
        acc = a_f * acc_ref[...] + jnp.dot(p_n.astype(BF16), vn_ref[...].astype(BF16), preferred_element_type=F32)
        o = acc / l_f
        lam = _diff_lambda(lq_ref[...], lam_init)
        for h in range(nh):
            b0 = h * 2 * sq
            d = o[b0:b0 + sq] - lam * o[b0 + sq:b0 + 2 * sq]
            ms = jnp.mean(d * d, axis=-1, keepdims=True)
            o_ref[h * sq:(h + 1) * sq, :] = (d * lax.rsqrt(ms + EPS) * g_ref[...] * (1.0 - lam_init)).astype(o_ref.dtype)


def _head_rows(x, nh):
    db, sq, _ = x.shape
    return x.reshape(db, sq, nh, HEAD_DIM).transpose(0, 2, 1, 3).reshape(db, nh * sq, HEAD_DIM)


def _from_head_rows(o, sq):
    db, rows, _ = o.shape
    nh = rows // sq
    return o.reshape(db, nh, sq, HEAD_DIM).transpose(0, 2, 1, 3).reshape(db, sq, nh * HEAD_DIM)


def _paged_diff(q, kv_new, cache, layer, page_table, diff_lambda, subln_g, lam_init):
    db, sq, w = q.shape
    nh = w // HEAD_DIM
    dd = diff_lambda.shape[-1]
    n_pages = page_table.shape[1]
    page = cache.shape[2]
    pp = max(c for c in (8, 4, 2, 1) if n_pages % c == 0)
    rows = nh * 2 * sq
    q4 = q.reshape(db, sq, nh, HEAD_DIM).transpose(0, 2, 1, 3) * ((dd ** -0.5) * LOG2E)
    lane = jnp.arange(HEAD_DIM) < dd
    q_rows = jnp.stack([jnp.where(lane, q4, 0.0), jnp.where(lane, 0.0, q4)], axis=2)
    q_rows = q_rows.reshape(db, rows, HEAD_DIM).astype(BF16)
    kn = kv_new[..., :w].reshape(db, sq * nh, HEAD_DIM)
    vn = kv_new[..., w:].reshape(db, sq * nh, HEAD_DIM)
    new_spec = pl.BlockSpec((None, sq * nh, HEAD_DIM), lambda bi, j, pt: (bi, 0, 0))
    grid_spec = pltpu.PrefetchScalarGridSpec(
        num_scalar_prefetch=1,
        grid=(db, n_pages // pp),
        in_specs=[pl.BlockSpec((None, rows, HEAD_DIM), lambda bi, j, pt: (bi, 0, 0)), new_spec, new_spec,
                  pl.BlockSpec(diff_lambda.shape, lambda bi, j, pt: (0, 0)),
                  pl.BlockSpec((1, HEAD_DIM), lambda bi, j, pt: (0, 0))] + _page_specs(layer, pp, page, nh),
        out_specs=pl.BlockSpec((None, nh * sq, HEAD_DIM), lambda bi, j, pt: (bi, 0, 0)),
        scratch_shapes=[pltpu.VMEM((rows, 1), F32), pltpu.VMEM((rows, 1), F32), pltpu.VMEM((rows, HEAD_DIM), F32)],
    )
    o = pl.pallas_call(
        functools.partial(_paged_diff_kernel, nh=nh, pp=pp, sq=sq, lam_init=lam_init),
        grid_spec=grid_spec,
        out_shape=jax.ShapeDtypeStruct((db, nh * sq, HEAD_DIM), BF16),
        compiler_params=_params(2),
        name="diff_paged",
    )(page_table, q_rows, kn, vn, diff_lambda, subln_g.reshape(1, HEAD_DIM), *([cache] * pp))
    return _from_head_rows(o, sq)


def _paged_moba_kernel(pt_ref, q_ref, qf_ref, kn_ref, vn_ref, *rest, nh, bps, sq):
    pages = rest[:2 * bps]
    o_ref, m_ref, l_ref, o_acc_ref, g_ref = rest[2 * bps:]
    j = pl.program_id(1)
    nb = m_ref.shape[0]
    rows = q_ref.shape[0]
    page = pages[0].shape[0]
    w = page * nh
    q = q_ref[...]
    qf = qf_ref[...]
    valid = _head_match(rows, w, sq, nh)
    valid = jnp.concatenate([valid, valid], axis=-1)

    for blk in range(bps):
        n = j * bps + blk
        p0, p1 = pages[2 * blk], pages[2 * blk + 1]
        kmean = (jnp.sum(p0[:, 0], axis=0) + jnp.sum(p1[:, 0], axis=0)) * (1.0 / (2 * page))
        kmean_rows = jnp.concatenate([jnp.broadcast_to(kmean[h:h + 1], (sq, HEAD_DIM)) for h in range(nh)], axis=0)
        g_ref[n] = jnp.sum(qf * kmean_rows, axis=-1, keepdims=True)
        k0, v0 = _flat_kv(p0)
        k1, v1 = _flat_kv(p1)
        s = jnp.concatenate([lax.dot_general(q, k0, _NT, preferred_element_type=F32),
                             lax.dot_general(q, k1, _NT, preferred_element_type=F32)], axis=-1)
        s = jnp.where(valid, s, NEG_INF)
        m = jnp.max(s, axis=-1, keepdims=True)
        p = jnp.exp2(s - m)
        m_ref[n] = m
        l_ref[n] = jnp.sum(p, axis=-1, keepdims=True)
        pb = p.astype(BF16)
        o_acc_ref[n] = (jnp.dot(pb[:, :w], v0, preferred_element_type=F32)
                        + jnp.dot(pb[:, w:], v1, preferred_element_type=F32))

    @pl.when(j == pl.num_programs(1) - 1)
    def _():
        n_iota = lax.broadcasted_iota(I32, (nb, rows, 1), 0)
        sel = _select_topk_lowest_index(g_ref[...], n_iota, nb, MOBA_TOPK, 0)
        m_blk = jnp.where(sel, m_ref[...], NEG_INF)
        s_n = lax.dot_general(q, kn_ref[...].astype(BF16), _NT, preferred_element_type=F32)
        s_n = jnp.where(_new_token_mask(rows, sq, sq, nh), s_n, NEG_INF)
        m_all = jnp.maximum(jnp.max(m_blk, axis=0), jnp.max(s_n, axis=-1, keepdims=True))
        p_n = jnp.exp2(s_n - m_all)
        e_blk = jnp.where(sel, jnp.exp2(m_blk - m_all[None]), 0.0)
        l_all = jnp.sum(e_blk * l_ref[...], axis=0) + jnp.sum(p_n, axis=-1, keepdims=True)
        o_all = (jnp.sum(e_blk * o_acc_ref[...], axis=0)
                 + jnp.dot(p_n.astype(BF16), vn_ref[...].astype(BF16), preferred_element_type=F32))
        o_ref[...] = (o_all / l_all).astype(o_ref.dtype)


def _paged_moba(q, kv_new, cache, layer, page_table):
    db, sq, w = q.shape
    nh = w // HEAD_DIM
    n_pages = page_table.shape[1]
    page = cache.shape[2]
    assert 2 * page == MOBA_BLOCK and n_pages % 2 == 0 and sq <= MOBA_BLOCK
    nb = n_pages // 2
    bps = max(c for c in (4, 2, 1) if nb % c == 0)
    pp = 2 * bps
    rows = nh * sq
    qf = _head_rows(q, nh)
    q_rows = (qf * ((HEAD_DIM ** -0.5) * LOG2E)).astype(BF16)
    kn = kv_new[..., :w].reshape(db, sq * nh, HEAD_DIM)
    vn = kv_new[..., w:].reshape(db, sq * nh, HEAD_DIM)
    row_spec = pl.BlockSpec((None, rows, HEAD_DIM), lambda bi, j, pt: (bi, 0, 0))
    new_spec = pl.BlockSpec((None, sq * nh, HEAD_DIM), lambda bi, j, pt: (bi, 0, 0))
    grid_spec = pltpu.PrefetchScalarGridSpec(
        num_scalar_prefetch=1,
        grid=(db, nb // bps),
        in_specs=[row_spec, row_spec, new_spec, new_spec] + _page_specs(layer, pp, page, nh),
        out_specs=row_spec,
        scratch_shapes=[pltpu.VMEM((nb, rows, 1), F32), pltpu.VMEM((nb, rows, 1), F32),
                        pltpu.VMEM((nb, rows, HEAD_DIM), F32), pltpu.VMEM((nb, rows, 1), F32)],
    )
    o = pl.pallas_call(
        functools.partial(_paged_moba_kernel, nh=nh, bps=bps, sq=sq),
        grid_spec=grid_spec,
        out_shape=jax.ShapeDtypeStruct((db, rows, HEAD_DIM), BF16),
        compiler_params=_params(2),
        name="moba_paged",
    )(page_table, q_rows, qf, kn, vn, *([cache] * pp))
    return _from_head_rows(o, sq)


def _shift_rows(x, d, fill):
    row = lax.broadcasted_iota(I32, x.shape, 0)
    return jnp.where(row >= d, pltpu.roll(x, d, 0), fill)


def _rglru_kernel(x_ref, h0_ref, c0_ref, cw_ref, cb_ref, wa_ref, ba_ref, wx_ref, bx_ref, lam_ref,
                  o_ref, hl_ref, cl_ref, hc_ref, tail_ref, *, cw_len):
    c = pl.program_id(2)
    nc = pl.num_programs(2)
    x = x_ref[...]
    tc = x.shape[0]
    nt = cw_len - 1

    @pl.when(c == 0)
    def _():
        hc_ref[...] = h0_ref[...]
        tail_ref[...] = c0_ref[...]

    tail = tail_ref[...]
    row = lax.broadcasted_iota(I32, x.shape, 0)
    cw = cw_ref[...]
    u = cb_ref[...] + x * cw[nt:nt + 1]
    for d in range(1, cw_len):
        fill = jnp.zeros_like(x)
        for r in range(d):
            fill = jnp.where(row == r, tail[nt - d + r:nt - d + r + 1], fill)
        u = u + _shift_rows(x, d, fill) * cw[nt - d:nt - d + 1]

    ub = u.astype(BF16)
    r_g = jax.nn.sigmoid(jnp.dot(ub, wa_ref[...].astype(BF16), preferred_element_type=F32) + ba_ref[...])
    i_g = jax.nn.sigmoid(jnp.dot(ub, wx_ref[...].astype(BF16), preferred_element_type=F32) + bx_ref[...])
    lam = lam_ref[...]
    log_sig = jnp.minimum(lam, 0.0) - jnp.log1p(jnp.exp(-jnp.abs(lam)))
    log_a = RGLRU_C * r_g * log_sig
    a = jnp.exp(log_a)
    bv = jnp.sqrt(-jnp.tanh(log_a) * (a * a + 1.0)) * (i_g * u)

    d = 1
    while d < tc:
        a_prev = _shift_rows(a, d, jnp.ones_like(a))
        b_prev = _shift_rows(bv, d, jnp.zeros_like(bv))
        bv = a * b_prev + bv
        a = a * a_prev
        d *= 2
    h = a * hc_ref[...] + bv
    o_ref[...] = h.astype(o_ref.dtype)
    hc_ref[...] = h[tc - 1:tc]
    if tc >= nt:
        tail_ref[...] = x[tc - nt:tc]
    else:
        tail_ref[...] = jnp.concatenate([tail[tc:], x], axis=0)

    @pl.when(c == nc - 1)
    def _():
        hl_ref[...] = hc_ref[...]
        cl_ref[...] = tail_ref[...]


def _rglru(xr, h0, conv0, conv_w, conv_b, w_a, b_a, w_x, b_x, lam):
    b, s, w = xr.shape
    nblk, bw, _ = w_a.shape
    assert bw == LANES
    cw_len = conv_w.shape[0]
    tc = _tile(s, 512)
    vec = lambda a: a.reshape(1, w)
    vspec = pl.BlockSpec((1, bw), lambda bi, n, c: (0, n))
    wspec = pl.BlockSpec((None, bw, bw), lambda bi, n, c: (n, 0, 0))
    return pl.pallas_call(
        functools.partial(_rglru_kernel, cw_len=cw_len),
        grid=(b, nblk, s // tc),
        in_specs=[pl.BlockSpec((None, tc, bw), lambda bi, n, c: (bi, c, n)),
                  pl.BlockSpec((None, 1, bw), lambda bi, n, c: (bi, 0, n)),
                  pl.BlockSpec((None, cw_len - 1, bw), lambda bi, n, c: (bi, 0, n)),
                  pl.BlockSpec((cw_len, bw), lambda bi, n, c: (0, n)),
                  vspec, wspec, vspec, wspec, vspec, vspec],
        out_specs=[pl.BlockSpec((None, tc, bw), lambda bi, n, c: (bi, c, n)),
                   pl.BlockSpec((None, 1, bw), lambda bi, n, c: (bi, 0, n)),
                   pl.BlockSpec((None, cw_len - 1, bw), lambda bi, n, c: (bi, 0, n))],
        out_shape=[jax.ShapeDtypeStruct((b, s, w), BF16),
                   jax.ShapeDtypeStruct((b, 1, w), F32),
                   jax.ShapeDtypeStruct((b, cw_len - 1, w), F32)],
        scratch_shapes=[pltpu.VMEM((1, bw), F32), pltpu.VMEM((cw_len - 1, bw), F32)],
        compiler_params=_params(3),
        name="rglru",
    )(xr, h0.reshape(b, 1, w), conv0, conv_w, vec(conv_b), w_a, vec(b_a), w_x, vec(b_x), vec(lam))


def _merge_kernel(a0_ref, a1_ref, a2_ref, g0_ref, g1_ref, g2_ref, w_ref, o_ref, wb_ref):
    @pl.when((pl.program_id(1) == 0) & (pl.program_id(2) == 0))
    def _():
        wb_ref[...] = w_ref[...].astype(BF16)

    acc = None
    for n, (a_ref, g_ref) in enumerate(((a0_ref, g0_ref), (a1_ref, g1_ref), (a2_ref, g2_ref))):
        y = jnp.dot(a_ref[...], wb_ref[n], preferred_element_type=F32) * jax.nn.sigmoid(g_ref[...])
        acc = y if acc is None else acc + y
    o_ref[...] = acc.astype(o_ref.dtype)


def _merge(oa, od, orr, gl, w_branch, layer):
    b, s, bw = oa.shape
    d = w_branch.shape[-1]
    tm = _tile(s, 512)
    tn = _tile(d, 512)
    nt = d // tn
    a_spec = pl.BlockSpec((None, tm, bw), lambda j, bi, i: (bi, i, 0))
    g_specs = [pl.BlockSpec((None, tm, tn), functools.partial(lambda j, bi, i, n: (bi, i, n * nt + j), n=n))
               for n in range(3)]
    return pl.pallas_call(
        _merge_kernel,
        grid=(nt, b, s // tm),
        in_specs=[a_spec, a_spec, a_spec] + g_specs
                 + [pl.BlockSpec((None, 3, bw, tn), lambda j, bi, i: (layer, 0, 0, j))],
        out_specs=pl.BlockSpec((None, tm, tn), lambda j, bi, i: (bi, i, j)),
        out_shape=jax.ShapeDtypeStruct((b, s, d), BF16),
        scratch_shapes=[pltpu.VMEM((3, bw, tn), BF16)],
        compiler_params=_params(3),
        name="branch_merge",
    )(oa, od, orr, gl, gl, gl, w_branch)


def _outproj_kernel(a_ref, w_ref, x_ref, g_ref, o_ref, wb_ref):
    @pl.when((pl.program_id(1) == 0) & (pl.program_id(2) == 0))
    def _():
        wb_ref[...] = w_ref[...].astype(BF16)

    y = jnp.dot(a_ref[...], wb_ref[...], preferred_element_type=F32)
    o_ref[...] = x_ref[...] + g_ref[...] * y


def _outproj(merged, w_o, layer, x, g1):
    b, s, d = x.shape
    tm = _tile(s, ROWS_PLAIN)
    tn = _tile(d, 1024)
    return pl.pallas_call(
        _outproj_kernel,
        grid=(d // tn, b, s // tm),
        in_specs=[pl.BlockSpec((None, tm, d), lambda j, bi, i: (bi, i, 0)),
                  pl.BlockSpec((None, d, tn), lambda j, bi, i: (layer, 0, j)),
                  pl.BlockSpec((None, tm, tn), lambda j, bi, i: (bi, i, j)),
                  pl.BlockSpec((None, 1, tn), lambda j, bi, i: (bi, 0, j))],
        out_specs=pl.BlockSpec((None, tm, tn), lambda j, bi, i: (bi, i, j)),
        out_shape=jax.ShapeDtypeStruct((b, s, d), F32),
        scratch_shapes=[pltpu.VMEM((d, tn), BF16)],
        compiler_params=_params(3),
        name="outproj",
    )(merged, w_o, x, g1)


def _stack_rows(rows):
    n = rows[0].shape[1]
    r_iota = lax.broadcasted_iota(I32, (len(rows), n), 0)
    out = jnp.zeros((len(rows), n), rows[0].dtype)
    for k, row in enumerate(rows):
        out = jnp.where(r_iota == k, row, out)
    return out


def _route_kernel(lg_ref, bias_ref, e_ref, w_ref, rank_ref, cnt_ref, carry_ref, *, n_exp):
    i = pl.program_id(0)

    @pl.when(i == 0)
    def _():
        carry_ref[...] = jnp.zeros(carry_ref.shape, F32)

    tile = lg_ref.shape[1]
    pg = n_exp // N_GROUPS
    scores = jax.nn.sigmoid(lg_ref[...])
    biased = scores + bias_ref[...]
    sc3 = scores.reshape(N_GROUPS, pg, tile)
    b3 = biased.reshape(N_GROUPS, pg, tile)
    j_iota = lax.broadcasted_iota(I32, b3.shape, 1)
    eid = lax.broadcasted_iota(I32, b3.shape, 0) * pg + j_iota

    m1 = jnp.max(b3, axis=1, keepdims=True)
    f1 = jnp.min(jnp.where(b3 == m1, j_iota, pg), axis=1, keepdims=True)
    m2 = jnp.max(jnp.where(j_iota == f1, NEG_INF, b3), axis=1, keepdims=True)
    grp = m1 + m2
    g_iota = lax.broadcasted_iota(I32, grp.shape, 0)
    g_sel = _select_topk_lowest_index(grp, g_iota, N_GROUPS, TOPK_GROUPS, 0)
    work = jnp.where(g_sel, b3, NEG_INF)

    e_rows, w_rows = [], []
    multi_hot = jnp.zeros(b3.shape, F32)
    for _ in range(TOP_K):
        m = jnp.max(jnp.max(work, axis=1, keepdims=True), axis=0, keepdims=True)
        first = jnp.min(jnp.min(jnp.where(work == m, eid, n_exp), axis=1, keepdims=True), axis=0, keepdims=True)
        pick = eid == first
        w_k = jnp.sum(jnp.sum(jnp.where(pick, sc3, 0.0), axis=1, keepdims=True), axis=0, keepdims=True)
        work = jnp.where(pick, NEG_INF, work)
        multi_hot = multi_hot + pick.astype(F32)
        e_rows.append(first.reshape(1, tile))
        w_rows.append(w_k.reshape(1, tile))
    e_idx = _stack_rows(e_rows)
    w = _stack_rows(w_rows)
    w = w / jnp.sum(w, axis=0, keepdims=True) * ROUTED_SCALE

    mh = multi_hot.reshape(n_exp, tile)
    r_i = lax.broadcasted_iota(I32, (tile, tile), 0)
    c_i = lax.broadcasted_iota(I32, (tile, tile), 1)
    before = (r_i < c_i).astype(BF16)
    base = carry_ref[:, 0:1] + jnp.dot(mh.astype(BF16), before, preferred_element_type=F32)
    e_sub = lax.broadcasted_iota(I32, (n_exp, tile), 0)
    ranks = [jnp.sum(jnp.where(e_sub == e_rows[k], base, 0.0), axis=0, keepdims=True) for k in range(TOP_K)]
    e_ref[...] = e_idx
    w_ref[...] = w
    rank_ref[...] = _stack_rows(ranks).astype(I32)
    carry_ref[...] = carry_ref[...] + jnp.sum(mh, axis=1, keepdims=True)
    cnt_ref[...] = carry_ref[...]


def _route(logits_t, bias):
    n_exp, t = logits_t.shape
    tile = 256
    out_spec = pl.BlockSpec((TOP_K, tile), lambda i: (0, i))
    return pl.pallas_call(
        functools.partial(_route_kernel, n_exp=n_exp),
        grid=(t // tile,),
        in_specs=[pl.BlockSpec((n_exp, tile), lambda i: (0, i)),
                  pl.BlockSpec((n_exp, 1), lambda i: (0, 0))],
        out_specs=[out_spec, out_spec, out_spec, pl.BlockSpec((n_exp, LANES), lambda i: (0, 0))],
        out_shape=[jax.ShapeDtypeStruct((TOP_K, t), I32), jax.ShapeDtypeStruct((TOP_K, t), F32),
                   jax.ShapeDtypeStruct((TOP_K, t), I32), jax.ShapeDtypeStruct((n_exp, LANES), F32)],
        scratch_shapes=[pltpu.VMEM((n_exp, LANES), F32)],
        compiler_params=_params(1),
        name="route",
    )(logits_t, bias.reshape(n_exp, 1))


def _slot_kernel(e_ref, rank_ref, ps_ref, o_ref):
    e_idx = e_ref[...]
    n_exp = ps_ref.shape[0]
    tile = e_idx.shape[1]
    e_sub = lax.broadcasted_iota(I32, (n_exp, tile), 0)
    ps = ps_ref[...]
    rows = [jnp.sum(jnp.where(e_sub == e_idx[k:k + 1], ps, 0), axis=0, keepdims=True) for k in range(TOP_K)]
    o_ref[...] = _stack_rows(rows) + rank_ref[...]


def _slots(e_idx, rank, p_start):
    _, t = e_idx.shape
    n_exp = p_start.shape[0]
    tile = 256
    spec = pl.BlockSpec((TOP_K, tile), lambda i: (0, i))
    return pl.pallas_call(
        _slot_kernel,
        grid=(t // tile,),
        in_specs=[spec, spec, pl.BlockSpec((n_exp, 1), lambda i: (0, 0))],
        out_specs=spec,
        out_shape=jax.ShapeDtypeStruct((TOP_K, t), I32),
        compiler_params=_params(1),
        name="slots",
    )(e_idx, rank, p_start.reshape(n_exp, 1))


def _zero_tails_kernel(blk_ref, o_ref):
    o_ref[...] = jnp.zeros(o_ref.shape, o_ref.dtype)


def _zero_tails(last_block, n_slots, dw, tm):
    n_exp = last_block.shape[0]
    grid_spec = pltpu.PrefetchScalarGridSpec(
        num_scalar_prefetch=1, grid=(n_exp,), in_specs=[],
        out_specs=pl.BlockSpec((tm, dw), lambda e, blk: (blk[e], 0)))
    return pl.pallas_call(
        _zero_tails_kernel, grid_spec=grid_spec,
        out_shape=jax.ShapeDtypeStruct((n_slots, dw), U32),
        compiler_params=_params(1),
        name="zero_tails",
    )(last_block)


def _row_copy(src_ref, src_row, dst_ref, dst_row, sem):
    return pltpu.make_async_copy(src_ref.at[pl.ds(src_row, 1), :], dst_ref.at[pl.ds(dst_row, 1), :], sem)


def _dispatch_kernel(pos_ref, h_ref, xs_in_ref, xs_ref, sem):
    del xs_in_ref
    tile = h_ref.shape[0]

    def issue(t, _):
        for k in range(TOP_K):
            _row_copy(h_ref, t, xs_ref, pos_ref[k, t], sem).start(priority=k % 2)
        return 0

    lax.fori_loop(0, tile, issue, 0)

    def drain(t, _):
        for k in range(TOP_K):
            _row_copy(h_ref, 0, xs_ref, 0, sem).wait()
        return 0

    lax.fori_loop(0, tile, drain, 0)


def _dispatch(pos_t, h_all, xs_init):
    t, dw = h_all.shape
    tile = 256
    return pl.pallas_call(
        _dispatch_kernel,
        grid=(t // tile,),
        in_specs=[pl.BlockSpec((TOP_K, tile), lambda i: (0, i), memory_space=pltpu.SMEM),
                  pl.BlockSpec((tile, dw), lambda i: (i, 0)),
                  pl.BlockSpec(memory_space=pl.ANY)],
        out_specs=pl.BlockSpec(memory_space=pl.ANY),
        out_shape=jax.ShapeDtypeStruct(xs_init.shape, xs_init.dtype),
        scratch_shapes=[pltpu.SemaphoreType.DMA],
        input_output_aliases={2: 0},
        compiler_params=_params(1),
        name="dispatch",
    )(pos_t, h_all, xs_init)


def _swiglu_packed(x_packed, wg_ref, wu_ref, wd_ref):
    lo, hi = _unpack_pair(x_packed)
    xl, xh = lo.astype(BF16), hi.astype(BF16)
    hw = xl.shape[1]
    g = (jnp.dot(xl, wg_ref[:hw], preferred_element_type=F32) + jnp.dot(xh, wg_ref[hw:], preferred_element_type=F32))
    u = (jnp.dot(xl, wu_ref[:hw], preferred_element_type=F32) + jnp.dot(xh, wu_ref[hw:], preferred_element_type=F32))
    mid = (g * jax.nn.sigmoid(g) * u).astype(BF16)
    return (jnp.dot(mid, wd_ref[:, :hw], preferred_element_type=F32),
            jnp.dot(mid, wd_ref[:, hw:], preferred_element_type=F32))


def _experts_kernel(be_ref, nu_ref, nxt_ref, slot_ref, x_ref, wg_ref, wu_ref, wd_ref, o_ref,
                    wgf_ref, wuf_ref, wdf_ref, wgb_ref, wub_ref, wdb_ref, sem, *, layer):
    b = pl.program_id(0)
    e = be_ref[b]
    new_expert = (b == 0) | (e != be_ref[jnp.maximum(b - 1, 0)])
    active = b < nu_ref[0]
    slot = slot_ref[e]

    def weight_copies(expert, s):
        return [pltpu.make_async_copy(w_ref.at[layer, expert], f_ref.at[s], sem.at[s, i])
                for i, (w_ref, f_ref) in enumerate(((wg_ref, wgf_ref), (wu_ref, wuf_ref), (wd_ref, wdf_ref)))]

    @pl.when(active & (b == 0))
    def _():
        for c in weight_copies(e, slot):
            c.start()

    @pl.when(active & new_expert)
    def _():
        for c in weight_copies(e, slot):
            c.wait()
        wgb_ref[...] = wgf_ref[slot].astype(BF16)
        wub_ref[...] = wuf_ref[slot].astype(BF16)
        wdb_ref[...] = wdf_ref[slot].astype(BF16)
        nxt = nxt_ref[e]

        @pl.when(nxt >= 0)
        def _():
            for c in weight_copies(nxt, 1 - slot):
                c.start()

    @pl.when(active)
    def _():
        y_lo, y_hi = _swiglu_packed(x_ref[...], wgb_ref, wub_ref, wdb_ref)
        o_ref[...] = _pack_pair(y_lo, y_hi)


def _experts(block_e, n_used, next_expert, slot_of_expert, xs, w_gate, w_up, w_down, layer, tm):
    n_slots, dw = xs.shape
    d, de = w_gate.shape[-2:]
    n_blocks = n_slots // tm

    def row_map(b, be, nu, nxt, slot):
        return (jnp.minimum(b, nu[0] - 1), 0)

    any_spec = pl.BlockSpec(memory_space=pl.ANY)
    grid_spec = pltpu.PrefetchScalarGridSpec(
        num_scalar_prefetch=4,
        grid=(n_blocks,),
        in_specs=[pl.BlockSpec((tm, dw), row_map), any_spec, any_spec, any_spec],
        out_specs=pl.BlockSpec((tm, dw), row_map),
        scratch_shapes=[pltpu.VMEM((2, d, de), F32), pltpu.VMEM((2, d, de), F32), pltpu.VMEM((2, de, d), F32),
                        pltpu.VMEM((d, de), BF16), pltpu.VMEM((d, de), BF16), pltpu.VMEM((de, d), BF16),
                        pltpu.SemaphoreType.DMA((2, 3))],
    )
    return pl.pallas_call(
        functools.partial(_experts_kernel, layer=layer), grid_spec=grid_spec,
        out_shape=jax.ShapeDtypeStruct((n_slots, dw), U32),
        compiler_params=_params(1),
        name="experts",
    )(block_e, n_used, next_expert, slot_of_expert, xs, w_gate, w_up, w_down)


def _combine_kernel(pos_ref, h_ref, x_ref, g_ref, w_ref, sg_ref, su_ref, sd_ref, ys_ref, o_ref,
                    buf_ref, sgb_ref, sub_ref, sdb_ref, sem):
    tile, hw = h_ref.shape

    @pl.when(pl.program_id(0) == 0)
    def _():
        sgb_ref[...] = sg_ref[...].astype(BF16)
        sub_ref[...] = su_ref[...].astype(BF16)
        sdb_ref[...] = sd_ref[...].astype(BF16)

    def issue(t, _):
        for k in range(TOP_K):
            _row_copy(ys_ref, pos_ref[k, t], buf_ref.at[k], t, sem).start(priority=k % 2)
        return 0

    lax.fori_loop(0, tile, issue, 0)

    acc_lo, acc_hi = _swiglu_packed(h_ref[...], sgb_ref, sub_ref, sdb_ref)

    def drain(t, _):
        for k in range(TOP_K):
            _row_copy(ys_ref, 0, buf_ref.at[k], 0, sem).wait()
        return 0

    lax.fori_loop(0, tile, drain, 0)

    w = w_ref[...]
    for k in range(TOP_K):
        lo, hi = _unpack_pair(buf_ref[k])
        acc_lo = acc_lo + w[:, k:k + 1] * lo
        acc_hi = acc_hi + w[:, k:k + 1] * hi
    g = g_ref[...]
    o_ref[:, :hw] = x_ref[:, :hw] + g[:, :hw] * acc_lo
    o_ref[:, hw:] = x_ref[:, hw:] + g[:, hw:] * acc_hi


def _combine(pos_t, h_all, row0, x, g2, w_tok, ys, s_gate, s_up, s_down, layer):
    t, d = x.shape
    dw = h_all.shape[1]
    de = s_gate.shape[-1]
    tile = 128
    b0 = row0 // tile
    g_rows = g2.shape[1]
    return pl.pallas_call(
        _combine_kernel,
        grid=(t // tile,),
        in_specs=[pl.BlockSpec((TOP_K, tile), lambda i: (0, b0 + i), memory_space=pltpu.SMEM),
                  pl.BlockSpec((tile, dw), lambda i: (b0 + i, 0)),
                  pl.BlockSpec((tile, d), lambda i: (i, 0)),
                  pl.BlockSpec((None, g_rows, d), lambda i: (i, 0, 0)),
                  pl.BlockSpec((tile, TOP_K), lambda i: (b0 + i, 0)),
                  pl.BlockSpec((None, d, de), lambda i: (layer, 0, 0)),
                  pl.BlockSpec((None, d, de), lambda i: (layer, 0, 0)),
                  pl.BlockSpec((None, de, d), lambda i: (layer, 0, 0)),
                  pl.BlockSpec(memory_space=pl.ANY)],
        out_specs=pl.BlockSpec((tile, d), lambda i: (i, 0)),
        out_shape=jax.ShapeDtypeStruct((t, d), F32),
        scratch_shapes=[pltpu.VMEM((TOP_K, tile, dw), U32), pltpu.VMEM((d, de), BF16),
                        pltpu.VMEM((d, de), BF16), pltpu.VMEM((de, d), BF16), pltpu.SemaphoreType.DMA],
        compiler_params=_params(1),
        name="combine",
    )(pos_t, h_all, x, g2, w_tok, s_gate, s_up, s_down, ys)


def _mixers(x, mods, layer, p, lam_init, past):
    b, s, d = x.shape
    w = p["moba_q_norm_g"].shape[-1] * (d // 256)
    sh1, sc1, g1 = mods
    h = _norm1(x, p["norm1_g"][layer], sc1, sh1)
    nh = w // HEAD_DIM
    tile_gain = lambda g: jnp.tile(g.reshape(1, -1), (1, nh))
    w_in = p["w_in"]
    tn = w
    prompt = past is None
    dd = p["diff_dim"]
    flat = (lambda a: a) if prompt else (lambda a: a.reshape(1, b * s, a.shape[-1]))
    unflat = (lambda a: a) if prompt else (lambda a: a.reshape(b, s, a.shape[-1]))
    hf = flat(h)
    qa = unflat(_inproj(hf, w_in, layer, 0, w, tn, F32, tile_gain(p["moba_q_norm_g"][layer]), LANES, 1))
    kv_a = unflat(_inproj(hf, w_in, layer, w, 2 * w, tn, F32, tile_gain(p["moba_k_norm_g"][layer]), LANES, 1, prompt))
    qd = unflat(_inproj(hf, w_in, layer, 3 * w, w, tn, F32, tile_gain(p["diff_q_norm_g"][layer]), dd, 1))
    kv_d = unflat(_inproj(hf, w_in, layer, 4 * w, 2 * w, tn, F32, tile_gain(p["diff_k_norm_g"][layer]), dd, 1, prompt))
    xr = unflat(_inproj(hf, w_in, layer, 6 * w, w, tn, F32))
    gl = unflat(_inproj(hf, w_in, layer, 7 * w, 3 * d, tn, F32))

    if prompt:
        kv_a, kvb_a = kv_a
        kv_d, kvb_d = kv_d
        oa = _moba_prompt(qa, kvb_a, _kmean(kv_a, w))
        od = _diff_prompt(qd, kvb_d, p["diff_lambda"][layer], p["diff_subln_g"][layer], lam_init)
        h0 = jnp.zeros((b, w), F32)
        conv0 = jnp.zeros((b, p["rglru_conv_w"].shape[1] - 1, w), F32)
    else:
        cache_a, cache_d, page_table, h0, conv0 = past
        oa = _paged_moba(qa, kv_a, cache_a, layer, page_table)
        od = _paged_diff(qd, kv_d, cache_d, layer, page_table, p["diff_lambda"][layer],
                         p["diff_subln_g"][layer], lam_init)
    orr, h_last, conv_last = _rglru(xr, h0, conv0, p["rglru_conv_w"][layer], p["rglru_conv_b"][layer],
                                    p["rglru_wa"][layer], p["rglru_ba"][layer], p["rglru_wx"][layer],
                                    p["rglru_bx"][layer], p["rglru_lambda"][layer])
    merged = unflat(_merge(flat(oa), flat(od), flat(orr), flat(gl), p["w_branch"], layer))
    x1 = _outproj(merged, p["w_o"], layer, x, g1)
    return x1, kv_a, kv_d, h_last.reshape(b, w), conv_last


def _moe(x1p, x1s, mods_p, mods_s, layer, p):
    bp, sp, d = x1p.shape
    bs, ss, _ = x1s.shape
    n_exp = p["router_w"].shape[-1]
    tp, ts = bp * sp, bs * ss
    dw = d // 2
    h2p, lgp = _norm2_router(x1p, p["norm2_g"][layer], mods_p[1], mods_p[0], p["router_w"][layer])
    h2s, lgs = _norm2_router(x1s, p["norm2_g"][layer], mods_s[1], mods_s[0], p["router_w"][layer])
    t_all = tp + -(-ts // 256) * 256
    pad = t_all - tp - ts
    h_all = jnp.concatenate([h2p.reshape(tp, dw), h2s.reshape(ts, dw), jnp.zeros((pad, dw), U32)], axis=0)
    lg_all = jnp.concatenate([lgp.reshape(tp, n_exp), lgs.reshape(ts, n_exp), jnp.zeros((pad, n_exp), F32)], axis=0)
    e_idx, w_t, rank, cnt = _route(lg_all.T, p["router_bias"][layer])

    tm = 256
    n_blocks = -(-(t_all * TOP_K) // tm) + n_exp
    counts = cnt[:, 0].astype(I32)
    padded = (counts + tm - 1) // tm * tm
    p_end = jnp.cumsum(padded)
    p_start = p_end - padded
    n_used = (p_end[-1:] // tm).astype(I32)
    block_row0 = jnp.arange(n_blocks, dtype=I32) * tm
    block_e = jnp.minimum(jnp.sum((p_end[None, :] <= block_row0[:, None]).astype(I32), axis=1), n_exp - 1)
    last_block = jnp.maximum(p_end // tm - 1, 0).astype(I32)
    pos_t = _slots(e_idx, rank, p_start.astype(I32))

    xs = _dispatch(pos_t, h_all, _zero_tails(last_block, n_blocks * tm, dw, tm))
    e_iota = jnp.arange(n_exp, dtype=I32)
    nonempty = counts > 0
    later = jnp.where(nonempty[None, :] & (e_iota[None, :] > e_iota[:, None]), e_iota[None, :], n_exp)
    next_expert = jnp.min(later, axis=1)
    next_expert = jnp.where(next_expert == n_exp, -1, next_expert).astype(I32)
    slot_of_expert = ((jnp.cumsum(nonempty.astype(I32)) - 1) % 2).astype(I32)
    ys = _experts(block_e, n_used, next_expert, slot_of_expert, xs,
                  p["expert_w_gate"], p["expert_w_up"], p["expert_w_down"], layer, tm)

    w_tok = w_t.T
    tile = 128
    shared = (p["shared_w_gate"], p["shared_w_up"], p["shared_w_down"], layer)
    g2p = jnp.broadcast_to(mods_p[2][:, None], (bp, sp // tile, 1, d)).reshape(tp // tile, 1, d)
    yp = _combine(pos_t, h_all, 0, x1p.reshape(tp, d), g2p, w_tok, ys, *shared)
    ts_pad = -(-ts // tile) * tile
    x1s_pad = jnp.concatenate([x1s.reshape(ts, d), jnp.zeros((ts_pad - ts, d), F32)], axis=0)
    g2s = jnp.broadcast_to(mods_s[2], (bs, ss, d)).reshape(ts, d)
    g2s = jnp.concatenate([g2s, jnp.zeros((ts_pad - ts, d), F32)], axis=0).reshape(ts_pad // tile, tile, d)
    ysamp = _combine(pos_t, h_all, tp, x1s_pad, g2s, w_tok, ys, *shared)
    return yp.reshape(bp, sp, d), ysamp[:ts].reshape(bs, ss, d)


def kernel(x_prompt, x_sample, cache_moba_kv, cache_diff_kv, state_rglru_h, state_rglru_conv, page_table, c_prompt, c_sample, w_ada, b_ada, norm1_g, norm2_g, w_in, moba_q_norm_g, moba_k_norm_g, diff_q_norm_g, diff_k_norm_g, diff_lambda, diff_subln_g, rglru_conv_w, rglru_conv_b, rglru_wa, rglru_ba, rglru_wx, rglru_bx, rglru_lambda, w_branch, w_o, router_w, router_bias, expert_w_gate, expert_w_up, expert_w_down, shared_w_gate, shared_w_up, shared_w_down):
    depth = w_ada.shape[0]
    bp, sp, d = x_prompt.shape
    bs, ss, _ = x_sample.shape
    nh, hd = cache_moba_kv.shape[-2:]
    assert hd == HEAD_DIM and sp % MOBA_BLOCK == 0 and (bp * sp) % 256 == 0
    p = dict(norm1_g=norm1_g, norm2_g=norm2_g, w_in=w_in, moba_q_norm_g=moba_q_norm_g,
             moba_k_norm_g=moba_k_norm_g, diff_dim=diff_q_norm_g.shape[-1],
             diff_q_norm_g=diff_q_norm_g.reshape(depth, -1), diff_k_norm_g=diff_k_norm_g.reshape(depth, -1),
             diff_lambda=diff_lambda, diff_subln_g=diff_subln_g, rglru_conv_w=rglru_conv_w,
             rglru_conv_b=rglru_conv_b, rglru_wa=rglru_wa, rglru_ba=rglru_ba, rglru_wx=rglru_wx,
             rglru_bx=rglru_bx, rglru_lambda=rglru_lambda, w_branch=w_branch, w_o=w_o,
             router_w=router_w, router_bias=router_bias, expert_w_gate=expert_w_gate,
             expert_w_up=expert_w_up, expert_w_down=expert_w_down, shared_w_gate=shared_w_gate,
             shared_w_up=shared_w_up, shared_w_down=shared_w_down)
    n_c = bp + bs
    r = -(-n_c // 8) * 8
    c_all = jnp.concatenate([c_prompt, c_sample, jnp.zeros((r - n_c, d), F32)], axis=0)
    mod = _adaln(c_all, w_ada, b_ada)

    def mods_of(layer, lo, hi):
        m = mod[layer, lo:hi][:, None, :]
        return [m[..., i * d:(i + 1) * d] for i in range(6)]

    yp, ys = x_prompt, x_sample
    outs = [[] for _ in range(8)]
    for layer in range(depth):
        lam_init = 0.8 - 0.6 * math.exp(-0.3 * layer)
        mp = mods_of(layer, 0, bp)
        ms = mods_of(layer, bp, n_c)
        x1p, kva_p, kvd_p, hl_p, cl_p = _mixers(yp, mp[0:3], layer, p, lam_init, None)
        past = (cache_moba_kv, cache_diff_kv, page_table, state_rglru_h[layer], state_rglru_conv[layer])
        x1s, kva_s, kvd_s, hl_s, cl_s = _mixers(ys, ms[0:3], layer, p, lam_init, past)
        yp, ys = _moe(x1p, x1s, mp[3:6], ms[3:6], layer, p)
        for lst, val in zip(outs, (kva_p.reshape(bp, sp, 2, nh, hd), kva_s.reshape(bs, ss, 2, nh, hd),
                                   kvd_p.reshape(bp, sp, 2, nh, hd), kvd_s.reshape(bs, ss, 2, nh, hd),
                                   hl_p, hl_s, cl_p, cl_s)):
            lst.append(val)
    return (yp, ys) + tuple(jnp.stack(o) for o in outs)
```
